```python
import jax, jax.numpy as jnp
from jax import lax
import numpy as np

D_MODEL = 2048
BATCH = 2
SEQ = 16384
DEPTH = 1

CHUNK = 64
Q_BLOCK = 128
MEM_LEN = 256
EPS = 1e-6

CONV_WIDTH = 1024
SHORT_CONV_K = 3
DIFF_HEADS = 8
DIFF_QK_DIM = 64
DIFF_V_DIM = 2 * DIFF_QK_DIM
DIFF_QK_WIDTH = DIFF_HEADS * 2 * DIFF_QK_DIM
DIFF_WIDTH = DIFF_HEADS * DIFF_V_DIM
CROSS_HEADS = 4
CROSS_HEAD_DIM = 256
CROSS_WIDTH = CROSS_HEADS * CROSS_HEAD_DIM
N_BRANCH = 3
D_FF = 5632
FFN_CONV_K = 3

IN_SPLITS = (CONV_WIDTH, CONV_WIDTH, CONV_WIDTH, DIFF_QK_WIDTH, DIFF_QK_WIDTH, DIFF_WIDTH, CROSS_WIDTH)
IN_WIDTH = sum(IN_SPLITS)
IN_OFFSETS = tuple(int(o) for o in np.cumsum(IN_SPLITS)[:-1])

kernel_name = "hybrid_gated_conv_diffattn_memxattn_block"


def rmsnorm(x, g):
    xf = x.astype(jnp.float32)
    y = xf * lax.rsqrt(jnp.mean(xf * xf, axis=-1, keepdims=True) + EPS)
    return (y * g.astype(jnp.float32)).astype(x.dtype)


def causal_dwconv(u, w):
    k = w.shape[0]
    return lax.conv_general_dilated(
        u, w[:, None, :].astype(u.dtype), window_strides=(1,), padding=[(k - 1, 0)],
        dimension_numbers=("NWC", "WIO", "NWC"), feature_group_count=u.shape[-1])


def short_conv_mixer(b, c, v, conv_w):
    return b * causal_dwconv(c * v, conv_w)


def diff_attention(q, k, v, lq1, lk1, lq2, lk2, g_sub, lambda_init):
    bsz, seq = q.shape[0], q.shape[1]
    n_blocks = seq // Q_BLOCK
    lam = (jnp.exp(jnp.sum(lq1.astype(jnp.float32) * lk1.astype(jnp.float32)))
           - jnp.exp(jnp.sum(lq2.astype(jnp.float32) * lk2.astype(jnp.float32)))
           + lambda_init)
    qh = q.transpose(0, 2, 3, 1, 4)
    kh = k.transpose(0, 2, 3, 1, 4)
    vh = v.transpose(0, 2, 1, 3)
    q_blocks = jnp.moveaxis(
        qh.reshape(bsz, DIFF_HEADS, 2, n_blocks, Q_BLOCK, DIFF_QK_DIM), 3, 0)
    key_chunk = jnp.arange(seq, dtype=jnp.int32) // CHUNK
    scale = DIFF_QK_DIM ** -0.5

    def block(args):
        qb, start = args
        s = jnp.einsum("bhcqd,bhckd->bhcqk", qb, kh,
                       preferred_element_type=jnp.float32) * scale
        q_chunk = (start + jnp.arange(Q_BLOCK, dtype=jnp.int32)) // CHUNK
        mask = key_chunk[None, :] <= q_chunk[:, None]
        p = jax.nn.softmax(jnp.where(mask, s, -jnp.inf), axis=-1)
        a = p[:, :, 0] - lam * p[:, :, 1]
        return jnp.einsum("bhqk,bhkd->bhqd", a.astype(vh.dtype), vh)

    starts = jnp.arange(n_blocks, dtype=jnp.int32) * Q_BLOCK
    o = lax.map(block, (q_blocks, starts))
    o = o.transpose(1, 0, 3, 2, 4).reshape(bsz, seq, DIFF_HEADS, DIFF_V_DIM)
    o = rmsnorm(o, g_sub) * (1.0 - lambda_init)
    return o.reshape(bsz, seq, DIFF_WIDTH)


def memory_cross_attention(q, mem_n, w_kv):
    bsz, seq = q.shape[0], q.shape[1]
    kv = mem_n @ w_kv
    k, v = jnp.split(kv, 2, axis=-1)
    k = k.reshape(bsz, -1, CROSS_HEADS, CROSS_HEAD_DIM)
    v = v.reshape(bsz, -1, CROSS_HEADS, CROSS_HEAD_DIM)
    s = jnp.einsum("bqhd,bkhd->bhqk", q, k, preferred_element_type=jnp.float32) * CROSS_HEAD_DIM ** -0.5
    p = jax.nn.softmax(s, axis=-1)
    o = jnp.einsum("bhqk,bkhd->bqhd", p.astype(v.dtype), v)
    return o.reshape(bsz, seq, CROSS_WIDTH)


def setup_inputs(seed: int = 0) -> dict:
    key = jax.random.key(seed)
    ks = jax.random.split(key, 26)

    def nrm(k, shape, scale):
        return jax.random.normal(k, shape, jnp.float32) * scale

    def gain(k, shape):
        return 1.0 + 0.05 * jax.random.normal(k, shape, jnp.float32)

    L, D = DEPTH, D_MODEL
    return {
        "x": nrm(ks[0], (BATCH, SEQ, D), 1.0),
        "mem": nrm(ks[1], (BATCH, MEM_LEN, D), 1.0),
        "g_mix_pre": gain(ks[2], (L, D)),
        "w_in": nrm(ks[3], (L, D, IN_WIDTH), D ** -0.5),
        "w_gate": nrm(ks[4], (L, D, N_BRANCH * D), D ** -0.5),
        "b_gate": nrm(ks[5], (L, N_BRANCH * D), 0.02),
        "conv_w": nrm(ks[6], (L, SHORT_CONV_K, CONV_WIDTH), SHORT_CONV_K ** -0.5),
        "w_conv_out": nrm(ks[7], (L, CONV_WIDTH, D), CONV_WIDTH ** -0.5),
        "lambda_q1": nrm(ks[8], (L, DIFF_QK_DIM), 0.1),
        "lambda_k1": nrm(ks[9], (L, DIFF_QK_DIM), 0.1),
        "lambda_q2": nrm(ks[10], (L, DIFF_QK_DIM), 0.1),
        "lambda_k2": nrm(ks[11], (L, DIFF_QK_DIM), 0.1),
        "g_diff_sub": gain(ks[12], (L, DIFF_V_DIM)),
        "w_diff_out": nrm(ks[13], (L, DIFF_WIDTH, D), DIFF_WIDTH ** -0.5),
        "g_mem": gain(ks[14], (L, D)),
        "w_mem_kv": nrm(ks[15], (L, D, 2 * CROSS_WIDTH), D ** -0.5),
        "w_cross_out": nrm(ks[16], (L, CROSS_WIDTH, D), CROSS_WIDTH ** -0.5),
        "w_o": nrm(ks[17], (L, D, D), D ** -0.5),
        "g_mix_post": gain(ks[18], (L, D)),
        "g_ffn_pre": gain(ks[19], (L, D)),
        "w_up": nrm(ks[20], (L, D, 2 * D_FF), D ** -0.5),
        "ffn_conv_w": nrm(ks[21], (L, FFN_CONV_K, 2 * D_FF), FFN_CONV_K ** -0.5),
        "w_down": nrm(ks[22], (L, D_FF, D), D_FF ** -0.5),
        "g_ffn_post": gain(ks[23], (L, D)),
    }


def reference(x, mem, g_mix_pre, w_in, w_gate, b_gate, conv_w, w_conv_out,
              lambda_q1, lambda_k1, lambda_q2, lambda_k2, g_diff_sub, w_diff_out,
              g_mem, w_mem_kv, w_cross_out, w_o, g_mix_post,
              g_ffn_pre, w_up, ffn_conv_w, w_down, g_ffn_post):
    bsz, seq = x.shape[0], x.shape[1]
    for l in range(DEPTH):
        lambda_init = 0.8 - 0.6 * float(np.exp(-0.3 * l))
        h = rmsnorm(x, g_mix_pre[l])
        proj = h @ w_in[l]
        cb, cc, cv, dq, dk, dv, xq = jnp.split(proj, IN_OFFSETS, axis=-1)

        y_conv = short_conv_mixer(cb, cc, cv, conv_w[l]) @ w_conv_out[l]

        y_diff = diff_attention(
            dq.reshape(bsz, seq, DIFF_HEADS, 2, DIFF_QK_DIM),
            dk.reshape(bsz, seq, DIFF_HEADS, 2, DIFF_QK_DIM),
            dv.reshape(bsz, seq, DIFF_HEADS, DIFF_V_DIM),
            lambda_q1[l], lambda_k1[l], lambda_q2[l], lambda_k2[l],
            g_diff_sub[l], lambda_init) @ w_diff_out[l]

        y_cross = memory_cross_attention(
            xq.reshape(bsz, seq, CROSS_HEADS, CROSS_HEAD_DIM),
            rmsnorm(mem, g_mem[l]), w_mem_kv[l]) @ w_cross_out[l]

        gates = jax.nn.sigmoid(h @ w_gate[l] + b_gate[l])
        g_a, g_b, g_c = jnp.split(gates, N_BRANCH, axis=-1)
        merged = g_a * y_conv + g_b * y_diff + g_c * y_cross
        x = x + rmsnorm(merged @ w_o[l], g_mix_post[l])

        h = rmsnorm(x, g_ffn_pre[l])
        u = causal_dwconv(h @ w_up[l], ffn_conv_w[l])
        gate, val = jnp.split(u, 2, axis=-1)
        x = x + rmsnorm((jax.nn.silu(gate) * val) @ w_down[l], g_ffn_post[l])
    return x
```

```python
import functools
import math

import jax
import jax.numpy as jnp
from jax import lax
from jax.experimental import pallas as pl
from jax.experimental.pallas import tpu as pltpu

EPS = 1e-6
CHUNK = 64
CONV_WIDTH = 1024
CONV_K = 3
DIFF_HEADS = 8
DIFF_QK_DIM = 64
DIFF_V_DIM = 2 * DIFF_QK_DIM
DIFF_WIDTH = DIFF_HEADS * DIFF_V_DIM
CROSS_HEADS = 4
CROSS_HEAD_DIM = 256
CROSS_WIDTH = CROSS_HEADS * CROSS_HEAD_DIM
N_BRANCH = 3

SUBLANES = 8
VMEM_LIMIT = 56 * 1024 * 1024

F32 = jnp.float32
BF16 = jnp.bfloat16


def _params(n_axes):
    return pltpu.CompilerParams(
        dimension_semantics=("arbitrary",) * n_axes, vmem_limit_bytes=VMEM_LIMIT)


def _dot(a, b):
    return jnp.dot(a, b, preferred_element_type=F32)


def _dot_nt(a, b):
    return lax.dot_general(a, b, (((1,), (1,)), ((), ())), preferred_element_type=F32)


def _tile(n, want):
    t = min(n, want)
    assert n % t == 0, (n, want)
    return t


def _rmsnorm_rows(x, g):
    ms = jnp.mean(x * x, axis=-1, keepdims=True)
    return x * lax.rsqrt(ms + EPS) * g


def _rmsnorm_kernel(x_ref, g_ref, o_ref):
    o_ref[...] = _rmsnorm_rows(x_ref[...], g_ref[...]).astype(o_ref.dtype)


def rmsnorm_bf16(x, g):
    m, d = x.shape
    tm = _tile(m, 512)
    return pl.pallas_call(
        _rmsnorm_kernel,
        grid=(m // tm,),
        in_specs=[pl.BlockSpec((tm, d), lambda i: (i, 0)),
                  pl.BlockSpec((1, d), lambda i: (0, 0))],
        out_specs=pl.BlockSpec((tm, d), lambda i: (i, 0)),
        out_shape=jax.ShapeDtypeStruct((m, d), BF16),
        compiler_params=_params(1),
        name="rmsnorm",
    )(x, g.reshape(1, d))


def _shift_rows(u, prev, s):
    rolled = pltpu.roll(u, s, axis=0)
    rprev = pltpu.roll(prev, s, axis=0)
    rows = lax.broadcasted_iota(jnp.int32, prev.shape, 0)
    first = jnp.where(rows < s, rprev, rolled[0:SUBLANES])
    return jnp.concatenate([first, rolled[SUBLANES:]], axis=0)


def _causal_conv3(u, prev, w):
    return (w[2:3] * u + w[1:2] * _shift_rows(u, prev, 1) + w[0:1] * _shift_rows(u, prev, 2))


def _load_carry(carry_ref, j, at_seq_start):
    @pl.when(at_seq_start)
    def _():
        carry_ref[j] = jnp.zeros(carry_ref.shape[1:], carry_ref.dtype)
    return carry_ref[j]


def _conv_mixer_kernel(h_ref, wb_ref, wc_ref, wv_ref, cw_ref, o_ref, carry_ref, *, tiles_per_seq):
    i = pl.program_id(0)
    j = pl.program_id(1)
    h = h_ref[...]
    u = _dot(h, wc_ref[...]) * _dot(h, wv_ref[...])
    prev = _load_carry(carry_ref, j, i % tiles_per_seq == 0)
    conv = _causal_conv3(u, prev, cw_ref[...])
    carry_ref[j] = u[u.shape[0] - SUBLANES:, :]
    o_ref[...] = (_dot(h, wb_ref[...]) * conv).astype(o_ref.dtype)


def conv_mixer(h, w_in, conv_w, seq):
    m, d = h.shape
    tm = _tile(seq, 1024)
    tn = 512
    nj = CONV_WIDTH // tn
    kern = functools.partial(_conv_mixer_kernel, tiles_per_seq=seq // tm)
    return pl.pallas_call(
        kern,
        grid=(m // tm, nj),
        in_specs=[pl.BlockSpec((tm, d), lambda i, j: (i, 0)),
                  pl.BlockSpec((d, tn), lambda i, j: (0, j)),
                  pl.BlockSpec((d, tn), lambda i, j: (0, nj + j)),
                  pl.BlockSpec((d, tn), lambda i, j: (0, 2 * nj + j)),
                  pl.BlockSpec((CONV_K, tn), lambda i, j: (0, j))],
        out_specs=pl.BlockSpec((tm, tn), lambda i, j: (i, j)),
        out_shape=jax.ShapeDtypeStruct((m, CONV_WIDTH), BF16),
        scratch_shapes=[pltpu.VMEM((nj, SUBLANES, tn), F32)],
        compiler_params=_params(2),
        name="conv_mixer",
    )(h, w_in, w_in, w_in, conv_w)


def _proj_kernel(h_ref, w_ref, o_ref):
    o_ref[...] = _dot(h_ref[...], w_ref[...]).astype(o_ref.dtype)


def project(h, w, col_block0, n_out, tm_want=1024, tn=1024):
    m, d = h.shape
    tm = _tile(m, tm_want)
    return pl.pallas_call(
        _proj_kernel,
        grid=(m // tm, n_out // tn),
        in_specs=[pl.BlockSpec((tm, d), lambda i, j: (i, 0)),
                  pl.BlockSpec((d, tn), lambda i, j: (0, col_block0 + j))],
        out_specs=pl.BlockSpec((tm, tn), lambda i, j: (i, j)),
        out_shape=jax.ShapeDtypeStruct((m, n_out), BF16),
        compiler_params=_params(2),
        name="project",
    )(h, w)


def _diff_attn_kernel(lam_ref, q_ref, k_ref, v_ref, gsub_ref, o_ref,
                      vt_ref, acc1, acc2, m1, l1, m2, l2, *, tq, lambda_init):
    qi = pl.program_id(2)
    seq = k_ref.shape[1]
    tk = tq

    @pl.when(qi == 0)
    def _():
        def body(c, carry):
            r = pl.multiple_of(c * tk, tk)
            vt_ref[:, pl.ds(r, tk)] = v_ref[0, pl.ds(r, tk), :].astype(F32).T.astype(BF16)
            return carry
        lax.fori_loop(0, seq // tk, body, 0)

    q = q_ref[0].astype(F32)
    lane = lax.broadcasted_iota(jnp.int32, q.shape, 1)
    qz = (jnp.where(lane < DIFF_QK_DIM, q, 0.0).astype(BF16),
          jnp.where(lane >= DIFF_QK_DIM, q, 0.0).astype(BF16))
    state = ((acc1, m1, l1), (acc2, m2, l2))

    for acc, m, l in state:
        acc[...] = jnp.zeros(acc.shape, F32)
        m[...] = jnp.full(m.shape, -jnp.inf, F32)
        l[...] = jnp.zeros(l.shape, F32)

    def process(k0, mask):
        k_t = k_ref[0, pl.ds(k0, tk), :]
        vt_t = vt_ref[:, pl.ds(k0, tk)]
        for c, (acc, m, l) in enumerate(state):
            s = _dot_nt(k_t, qz[c])
            if mask is not None:
                s = jnp.where(mask, s, -jnp.inf)
            m_old = m[...]
            m_new = jnp.maximum(m_old, jnp.max(s, axis=0, keepdims=True))
            alpha = jnp.exp(m_old - m_new)
            p = jnp.exp(s - m_new)
            l[...] = alpha * l[...] + jnp.sum(p, axis=0, keepdims=True)
            acc[...] = alpha * acc[...] + _dot(vt_t, p.astype(BF16))
            m[...] = m_new

    key_chunk = lax.broadcasted_iota(jnp.int32, (tk, tq), 0) // CHUNK
    qry_chunk = lax.broadcasted_iota(jnp.int32, (tk, tq), 1) // CHUNK
    process(pl.multiple_of(qi * tq, tq), key_chunk <= qry_chunk)

    def body(kj, carry):
        process(pl.multiple_of(kj * tk, tk), None)
        return carry
    lax.fori_loop(0, qi, body, 0)

    lv = lam_ref[...]
    lam = (jnp.exp(jnp.sum(lv[0:1] * lv[1:2], axis=-1, keepdims=True))
           - jnp.exp(jnp.sum(lv[2:3] * lv[3:4], axis=-1, keepdims=True)) + lambda_init)
    o = acc1[...] / l1[...] - lam * (acc2[...] / l2[...])
    ms = jnp.mean(o * o, axis=0, keepdims=True)
    y = o * lax.rsqrt(ms + EPS) * gsub_ref[...]
    y = y * (1.0 - lambda_init)
    o_ref[0] = y.T.astype(o_ref.dtype)


def diff_attention(qkvx, lam_rows, g_sub, bsz, seq, lambda_init):
    tq = _tile(seq, 512)
    nq = seq // tq
    dv = DIFF_V_DIM
    kern = functools.partial(_diff_attn_kernel, tq=tq, lambda_init=lambda_init)
    return pl.pallas_call(
        kern,
        grid=(bsz, DIFF_HEADS, nq),
        in_specs=[pl.BlockSpec((4, DIFF_QK_DIM), lambda b, h, i: (0, 0)),
                  pl.BlockSpec((1, tq, dv), lambda b, h, i: (b, i, h)),
                  pl.BlockSpec((1, seq, dv), lambda b, h, i: (b, 0, DIFF_HEADS + h)),
                  pl.BlockSpec((1, seq, dv), lambda b, h, i: (b, 0, 2 * DIFF_HEADS + h)),
                  pl.BlockSpec((dv, 1), lambda b, h, i: (0, 0))],
        out_specs=pl.BlockSpec((1, tq, dv), lambda b, h, i: (b, i, h)),
        out_shape=jax.ShapeDtypeStruct((bsz, seq, DIFF_WIDTH), BF16),
        scratch_shapes=[pltpu.VMEM((dv, seq), BF16),
                        pltpu.VMEM((dv, tq), F32), pltpu.VMEM((dv, tq), F32),
                        pltpu.VMEM((1, tq), F32), pltpu.VMEM((1, tq), F32),
                        pltpu.VMEM((1, tq), F32), pltpu.VMEM((1, tq), F32)],
        compiler_params=_params(3),
        name="diff_attention",
    )(lam_rows, qkvx, qkvx, qkvx, g_sub.reshape(dv, 1))


def _cross_attn_kernel(xq_ref, kv_ref, o_ref):
    hd = CROSS_HEAD_DIM
    for h in range(CROSS_HEADS):
        q = xq_ref[0, :, h * hd:(h + 1) * hd]
        k = kv_ref[0, :, h * hd:(h + 1) * hd]
        v = kv_ref[0, :, CROSS_WIDTH + h * hd:CROSS_WIDTH + (h + 1) * hd]
        s = _dot_nt(q, k) * (hd ** -0.5)
        p = jnp.exp(s - jnp.max(s, axis=-1, keepdims=True))
        l = jnp.sum(p, axis=-1, keepdims=True)
        o_ref[0, :, h * hd:(h + 1) * hd] = (_dot(p.astype(BF16), v) / l).astype(o_ref.dtype)


def cross_attention(qkvx, kv, bsz, seq):
    tm = _tile(seq, 512)
    mem_len = kv.shape[1]
    return pl.pallas_call(
        _cross_attn_kernel,
        grid=(bsz, seq // tm),
        in_specs=[pl.BlockSpec((1, tm, CROSS_WIDTH), lambda b, i: (b, i, 3)),
                  pl.BlockSpec((1, mem_len, 2 * CROSS_WIDTH), lambda b, i: (b, 0, 0))],
        out_specs=pl.BlockSpec((1, tm, CROSS_WIDTH), lambda b, i: (b, i, 0)),
        out_shape=jax.ShapeDtypeStruct((bsz, seq, CROSS_WIDTH), BF16),
        compiler_params=_params(2),
        name="cross_attention",
    )(qkvx, kv)


def _merge_kernel(h_ref, ya_ref, yd_ref, yx_ref, wga_ref, wgb_ref, wgc_ref,
                  ba_ref, bb_ref, bc_ref, wa_ref, wd_ref, wx_ref, o_ref):
    h = h_ref[...]
    merged = jax.nn.sigmoid(_dot(h, wga_ref[...]) + ba_ref[...]) * _dot(ya_ref[...], wa_ref[...])
    merged += jax.nn.sigmoid(_dot(h, wgb_ref[...]) + bb_ref[...]) * _dot(yd_ref[...], wd_ref[...])
    merged += jax.nn.sigmoid(_dot(h, wgc_ref[...]) + bc_ref[...]) * _dot(yx_ref[...], wx_ref[...])
    o_ref[...] = merged.astype(o_ref.dtype)


def gated_merge(h, ya, yd, yx, w_gate, b_gate, w_a, w_d, w_x):
    m, d = h.shape
    tm = _tile(m, 1024)
    tn = 512
    nj = d // tn
    act = lambda width: pl.BlockSpec((tm, width), lambda i, j: (i, 0))
    gate_w = lambda br: pl.BlockSpec((d, tn), lambda i, j: (0, br * nj + j))
    gate_b = lambda br: pl.BlockSpec((1, tn), lambda i, j: (0, br * nj + j))
    out_w = lambda width: pl.BlockSpec((width, tn), lambda i, j: (0, j))
    return pl.pallas_call(
        _merge_kernel,
        grid=(m // tm, nj),
        in_specs=[act(d), act(CONV_WIDTH), act(DIFF_WIDTH), act(CROSS_WIDTH),
                  gate_w(0), gate_w(1), gate_w(2), gate_b(0), gate_b(1), gate_b(2),
                  out_w(CONV_WIDTH), out_w(DIFF_WIDTH), out_w(CROSS_WIDTH)],
        out_specs=pl.BlockSpec((tm, tn), lambda i, j: (i, j)),
        out_shape=jax.ShapeDtypeStruct((m, d), BF16),
        compiler_params=_params(2),
        name="gated_merge",
    )(h, ya, yd, yx, w_gate, w_gate, w_gate, b_gate, b_gate, b_gate, w_a, w_d, w_x)


def _out_proj_kernel(z_ref, w_ref, x_ref, gpost_ref, gpre_ref, x1_ref, h2_ref):
    y = _dot(z_ref[...], w_ref[...])
    x1 = x_ref[...] + _rmsnorm_rows(y, gpost_ref[...])
    x1_ref[...] = x1
    h2_ref[...] = _rmsnorm_rows(x1, gpre_ref[...]).astype(h2_ref.dtype)


def out_proj_residual(z, w_o, x, g_post, g_next_pre):
    m, d = x.shape
    tm = _tile(m, 512)
    row = lambda: pl.BlockSpec((tm, d), lambda i: (i, 0))
    vec = lambda: pl.BlockSpec((1, d), lambda i: (0, 0))
    return pl.pallas_call(
        _out_proj_kernel,
        grid=(m // tm,),
        in_specs=[row(), pl.BlockSpec((d, d), lambda i: (0, 0)), row(), vec(), vec()],
        out_specs=[row(), row()],
        out_shape=[jax.ShapeDtypeStruct((m, d), F32), jax.ShapeDtypeStruct((m, d), BF16)],
        compiler_params=_params(1),
        name="out_proj_residual",
    )(z, w_o, x, g_post.reshape(1, d), g_next_pre.reshape(1, d))


def _ffn_up_kernel(h_ref, wg_ref, wv_ref, cwg_ref, cwv_ref, o_ref, carry_g, carry_v, *, tiles_per_seq):
    i = pl.program_id(0)
    j = pl.program_id(1)
    h = h_ref[...]
    start = i % tiles_per_seq == 0
    halves = []
    for w_ref, cw_ref, carry_ref in ((wg_ref, cwg_ref, carry_g), (wv_ref, cwv_ref, carry_v)):
        z = _dot(h, w_ref[...])
        prev = _load_carry(carry_ref, j, start)
        halves.append(_causal_conv3(z, prev, cw_ref[...]))
        carry_ref[j] = z[z.shape[0] - SUBLANES:, :]
    gate, val = halves
    o_ref[...] = (gate * jax.nn.sigmoid(gate) * val).astype(o_ref.dtype)


def ffn_up(h, w_up, conv_w, seq):
    m, d = h.shape
    d_ff = w_up.shape[1] // 2
    tm = _tile(seq, 1024)
    tn = 512
    nj = d_ff // tn
    kern = functools.partial(_ffn_up_kernel, tiles_per_seq=seq // tm)
    return pl.pallas_call(
        kern,
        grid=(m // tm, nj),
        in_specs=[pl.BlockSpec((tm, d), lambda i, j: (i, 0)),
                  pl.BlockSpec((d, tn), lambda i, j: (0, j)),
                  pl.BlockSpec((d, tn), lambda i, j: (0, nj + j)),
                  pl.BlockSpec((CONV_K, tn), lambda i, j: (0, j)),
                  pl.BlockSpec((CONV_K, tn), lambda i, j: (0, nj + j))],
        out_specs=pl.BlockSpec((tm, tn), lambda i, j: (i, j)),
        out_shape=jax.ShapeDtypeStruct((m, d_ff), BF16),
        scratch_shapes=[pltpu.VMEM((nj, SUBLANES, tn), F32), pltpu.VMEM((nj, SUBLANES, tn), F32)],
        compiler_params=_params(2),
        name="ffn_up",
    )(h, w_up, w_up, conv_w, conv_w)


def _ffn_down_kernel(a_ref, w_ref, x_ref, g_ref, o_ref, acc_ref):
    k = pl.program_id(1)

    @pl.when(k == 0)
    def _():
        acc_ref[...] = jnp.zeros(acc_ref.shape, F32)

    acc_ref[...] += _dot(a_ref[...], w_ref[...])

    @pl.when(k == pl.num_programs(1) - 1)
    def _():
        o_ref[...] = x_ref[...] + _rmsnorm_rows(acc_ref[...], g_ref[...])


def ffn_down(a, w_down, x, g_post):
    m, d = x.shape
    d_ff = a.shape[1]
    tm = _tile(m, 512)
    tk = 1408
    row = lambda: pl.BlockSpec((tm, d), lambda i, k: (i, 0))
    return pl.pallas_call(
        _ffn_down_kernel,
        grid=(m // tm, d_ff // tk),
        in_specs=[pl.BlockSpec((tm, tk), lambda i, k: (i, k)),
                  pl.BlockSpec((tk, d), lambda i, k: (k, 0)),
                  row(), pl.BlockSpec((1, d), lambda i, k: (0, 0))],
        out_specs=row(),
        out_shape=jax.ShapeDtypeStruct((m, d), F32),
        scratch_shapes=[pltpu.VMEM((tm, d), F32)],
        compiler_params=_params(2),
        name="ffn_down",
    )(a, w_down, x, g_post.reshape(1, d))


def kernel(x, mem, g_mix_pre, w_in, w_gate, b_gate, conv_w, w_conv_out, lambda_q1, lambda_k1, lambda_q2, lambda_k2, g_diff_sub, w_diff_out, g_mem, w_mem_kv, w_cross_out, w_o, g_mix_post, g_ffn_pre, w_up, ffn_conv_w, w_down, g_ffn_post):
    bsz, seq, d = x.shape
    mem_len = mem.shape[1]
    depth = w_in.shape[0]
    m = bsz * seq
    qkvx_width = 2 * DIFF_HEADS * 2 * DIFF_QK_DIM + DIFF_WIDTH + CROSS_WIDTH
    q_lo = N_BRANCH * CONV_WIDTH
    col = jnp.arange(w_in.shape[-1])
    in_scale = jnp.where((col >= q_lo) & (col < q_lo + DIFF_HEADS * 2 * DIFF_QK_DIM),
                         DIFF_QK_DIM ** -0.5, 1.0).astype(F32)

    xf = x.reshape(m, d)
    h = rmsnorm_bf16(xf, g_mix_pre[0])
    for l in range(depth):
        lambda_init = 0.8 - 0.6 * math.exp(-0.3 * l)
        w_in_b = (w_in[l] * in_scale).astype(BF16)
        ya = conv_mixer(h, w_in_b, conv_w[l], seq)
        qkvx = project(h, w_in_b, q_lo // 1024, qkvx_width).reshape(bsz, seq, qkvx_width)

        lam_rows = jnp.stack([lambda_q1[l], lambda_k1[l], lambda_q2[l], lambda_k2[l]])
        yd = diff_attention(qkvx, lam_rows, g_diff_sub[l], bsz, seq, lambda_init)

        mem_n = rmsnorm_bf16(mem.reshape(bsz * mem_len, d), g_mem[l])
        kv = project(mem_n, w_mem_kv[l].astype(BF16), 0, 2 * CROSS_WIDTH, tm_want=bsz * mem_len)
        yx = cross_attention(qkvx, kv.reshape(bsz, mem_len, 2 * CROSS_WIDTH), bsz, seq)

        merged = gated_merge(h, ya, yd.reshape(m, DIFF_WIDTH), yx.reshape(m, CROSS_WIDTH),
                             w_gate[l].astype(BF16), b_gate[l].reshape(1, -1),
                             w_conv_out[l].astype(BF16), w_diff_out[l].astype(BF16),
                             w_cross_out[l].astype(BF16))
        xf, h2 = out_proj_residual(merged, w_o[l].astype(BF16), xf, g_mix_post[l], g_ffn_pre[l])

        act = ffn_up(h2, w_up[l].astype(BF16), ffn_conv_w[l], seq)
        xf = ffn_down(act, w_down[l].astype(BF16), xf, g_ffn_post[l])
        if l + 1 < depth:
            h = rmsnorm_bf16(xf, g_mix_pre[l + 1])
    return xf.reshape(bsz, seq, d)
```

```python
import functools
import math

import jax
import jax.numpy as jnp
from jax import lax
from jax.experimental import pallas as pl
from jax.experimental.pallas import tpu as pltpu

EPS = 1e-6
CHUNK = 64
CONV_WIDTH = 1024
CONV_K = 3
DIFF_HEADS = 8
DIFF_QK_DIM = 64
DIFF_V_DIM = 2 * DIFF_QK_DIM
DIFF_QK_WIDTH = DIFF_HEADS * 2 * DIFF_QK_DIM
DIFF_WIDTH = DIFF_HEADS * DIFF_V_DIM
CROSS_HEADS = 4
CROSS_HEAD_DIM = 256
CROSS_WIDTH = CROSS_HEADS * CROSS_HEAD_DIM
N_BRANCH = 3
LOG2E = math.log2(math.e)

SUBLANES = 8
BF16_ROWS = 16
VMEM_LIMIT = 56 * 1024 * 1024

F32 = jnp.float32
BF16 = jnp.bfloat16


def _params(n_axes):
    return pltpu.CompilerParams(
        dimension_semantics=("arbitrary",) * n_axes, vmem_limit_bytes=VMEM_LIMIT)


def _dot(a, b):
    return jnp.dot(a, b, preferred_element_type=F32)


def _dot_nt(a, b):
    return lax.dot_general(a, b, (((1,), (1,)), ((), ())), preferred_element_type=F32)


def _tile(n, want):
    t = min(n, want)
    assert n % t == 0, (n, want)
    return t


def _rmsnorm_rows(x, g):
    ms = jnp.mean(x * x, axis=-1, keepdims=True)
    return x * lax.rsqrt(ms + EPS) * g


def _rmsnorm_kernel(x_ref, g_ref, o_ref):
    o_ref[...] = _rmsnorm_rows(x_ref[...], g_ref[...]).astype(o_ref.dtype)


def rmsnorm_bf16(x, g):
    m, d = x.shape
    tm = _tile(m, 512)
    return pl.pallas_call(
        _rmsnorm_kernel,
        grid=(m // tm,),
        in_specs=[pl.BlockSpec((tm, d), lambda i: (i, 0)),
                  pl.BlockSpec((1, d), lambda i: (0, 0))],
        out_specs=pl.BlockSpec((tm, d), lambda i: (i, 0)),
        out_shape=jax.ShapeDtypeStruct((m, d), BF16),
        compiler_params=_params(1),
        name="rmsnorm",
    )(x, g.reshape(1, d))


def _shift_rows(u, prev, s):
    rolled = pltpu.roll(u, s, axis=0)
    rprev = pltpu.roll(prev, s, axis=0)
    rows = lax.broadcasted_iota(jnp.int32, prev.shape, 0)
    first = jnp.where(rows < s, rprev, rolled[0:SUBLANES])
    return jnp.concatenate([first, rolled[SUBLANES:]], axis=0)


def _causal_conv3(u, prev, w):
    return (w[2:3] * u + w[1:2] * _shift_rows(u, prev, 1) + w[0:1] * _shift_rows(u, prev, 2))


def _load_carry(carry_ref, j, at_seq_start):
    @pl.when(at_seq_start)
    def _():
        carry_ref[j] = jnp.zeros(carry_ref.shape[1:], carry_ref.dtype)
    return carry_ref[j]


def _conv_mixer_kernel(h_ref, wb_ref, wc_ref, wv_ref, cw_ref, o_ref, carry_ref, *, tiles_per_seq):
    i = pl.program_id(0)
    j = pl.program_id(1)
    h = h_ref[...]
    u = _dot(h, wc_ref[...]) * _dot(h, wv_ref[...])
    prev = _load_carry(carry_ref, j, i % tiles_per_seq == 0)
    conv = _causal_conv3(u, prev, cw_ref[...])
    carry_ref[j] = u[u.shape[0] - SUBLANES:, :]
    o_ref[...] = (_dot(h, wb_ref[...]) * conv).astype(o_ref.dtype)


def conv_mixer(h, w_in, conv_w, seq):
    m, d = h.shape
    tm = _tile(seq, 1024)
    tn = 512
    nj = CONV_WIDTH // tn
    kern = functools.partial(_conv_mixer_kernel, tiles_per_seq=seq // tm)
    return pl.pallas_call(
        kern,
        grid=(m // tm, nj),
        in_specs=[pl.BlockSpec((tm, d), lambda i, j: (i, 0)),
                  pl.BlockSpec((d, tn), lambda i, j: (0, j)),
                  pl.BlockSpec((d, tn), lambda i, j: (0, nj + j)),
                  pl.BlockSpec((d, tn), lambda i, j: (0, 2 * nj + j)),
                  pl.BlockSpec((CONV_K, tn), lambda i, j: (0, j))],
        out_specs=pl.BlockSpec((tm, tn), lambda i, j: (i, j)),
        out_shape=jax.ShapeDtypeStruct((m, CONV_WIDTH), BF16),
        scratch_shapes=[pltpu.VMEM((nj, SUBLANES, tn), F32)],
        compiler_params=_params(2),
        name="conv_mixer",
    )(h, w_in, w_in, w_in, conv_w)


def _proj_kernel(h_ref, w_ref, scale_ref, o_ref):
    o_ref[...] = (_dot(h_ref[...], w_ref[...]) * scale_ref[...]).astype(o_ref.dtype)


def project(h, w, col_scale, col_block0, tm_want=1024, tn=1024):
    m, d = h.shape
    n_out = col_scale.shape[0]
    tm = _tile(m, tm_want)
    return pl.pallas_call(
        _proj_kernel,
        grid=(m // tm, n_out // tn),
        in_specs=[pl.BlockSpec((tm, d), lambda i, j: (i, 0)),
                  pl.BlockSpec((d, tn), lambda i, j: (0, col_block0 + j)),
                  pl.BlockSpec((1, tn), lambda i, j: (0, j))],
        out_specs=pl.BlockSpec((tm, tn), lambda i, j: (i, j)),
        out_shape=jax.ShapeDtypeStruct((m, n_out), BF16),
        compiler_params=_params(2),
        name="project",
    )(h, w, col_scale.reshape(1, n_out))


def _col_max(s):
    r = s.shape[0]
    while r > SUBLANES:
        r //= 2
        s = jnp.maximum(s[:r], s[r:])
    return jnp.max(s, axis=0, keepdims=True)


def _diff_attn_kernel(lam_ref, q_ref, k_ref, v_ref, gsub_ref, o_ref,
                      vt_ref, s_ref, p_ref, alpha_ref, m_ref, acc_ref, *, tq, tk, lambda_init):
    qi = pl.program_id(2)
    seq = k_ref.shape[1]
    dv = DIFF_V_DIM
    assert tq == 2 * tk

    @pl.when(qi == 0)
    def _():
        vt_ref[dv:, :] = jnp.ones((BF16_ROWS, seq), BF16)

        def body(c, carry):
            r = pl.multiple_of(c * tq, tq)
            vt_ref[0:dv, pl.ds(r, tq)] = v_ref[0, pl.ds(r, tq), :].astype(F32).T.astype(BF16)
            return carry
        lax.fori_loop(0, seq // tq, body, 0)

    q = q_ref[0].astype(F32)
    lane = lax.broadcasted_iota(jnp.int32, q.shape, 1)
    qz = (jnp.where(lane < DIFF_QK_DIM, q, 0.0).astype(BF16),
          jnp.where(lane >= DIFF_QK_DIM, q, 0.0).astype(BF16))

    def scores(t, slot):
        k_t = k_ref[0, pl.ds(pl.multiple_of(t * tk, tk), tk), :]
        for c in range(2):
            s_ref[slot, c] = _dot_nt(k_t, qz[c])

    def softmax(slot, mask):
        for c in range(2):
            s = s_ref[slot, c]
            if mask is not None:
                s = jnp.where(mask, s, -jnp.inf)
            m_old = m_ref[c]
            m_new = jnp.maximum(m_old, _col_max(s))
            alpha_ref[slot, c] = jnp.exp2(m_old - m_new)
            p_ref[slot, c] = jnp.exp2(s - m_new).astype(BF16)
            m_ref[c] = m_new

    def values(t, slot):
        vt_t = vt_ref[:, pl.ds(pl.multiple_of(t * tk, tk), tk)]
        for c in range(2):
            acc_ref[c] = alpha_ref[slot, c] * acc_ref[c] + _dot(vt_t, p_ref[slot, c])

    acc_ref[...] = jnp.zeros(acc_ref.shape, F32)
    m_ref[...] = jnp.full(m_ref.shape, -jnp.inf, F32)
    p_ref[1] = jnp.zeros(p_ref.shape[1:], BF16)
    alpha_ref[1] = jnp.ones(alpha_ref.shape[1:], F32)
    scores(0, 0)

    def pair(j, carry):
        t0 = 2 * j
        values(jnp.maximum(t0 - 1, 0), 1)
        scores(t0 + 1, 1)
        softmax(0, None)
        values(t0, 0)
        scores(t0 + 2, 0)
        softmax(1, None)
        return carry
    lax.fori_loop(0, qi, pair, 0)

    t0 = 2 * qi
    key_chunk = lax.broadcasted_iota(jnp.int32, (tk, tq), 0) // CHUNK
    qry_chunk = lax.broadcasted_iota(jnp.int32, (tk, tq), 1) // CHUNK
    values(jnp.maximum(t0 - 1, 0), 1)
    scores(t0 + 1, 1)
    softmax(0, key_chunk <= qry_chunk)
    values(t0, 0)
    softmax(1, key_chunk + tk // CHUNK <= qry_chunk)
    values(t0 + 1, 1)

    lv = lam_ref[...]
    lam = (jnp.exp(jnp.sum(lv[0:1] * lv[1:2], axis=-1, keepdims=True))
           - jnp.exp(jnp.sum(lv[2:3] * lv[3:4], axis=-1, keepdims=True)) + lambda_init)
    a1 = acc_ref[0]
    a2 = acc_ref[1]
    o = a1[0:dv] / a1[dv:dv + 1] - lam * (a2[0:dv] / a2[dv:dv + 1])
    ms = jnp.mean(o * o, axis=0, keepdims=True)
    y = o * lax.rsqrt(ms + EPS) * gsub_ref[...]
    y = y * (1.0 - lambda_init)
    o_ref[0] = y.T.astype(o_ref.dtype)


def diff_attention(qkvx, lam_rows, g_sub, bsz, seq, lambda_init):
    tq = _tile(seq, 512)
    tk = tq // 2
    nq = seq // tq
    dv = DIFF_V_DIM
    kern = functools.partial(_diff_attn_kernel, tq=tq, tk=tk, lambda_init=lambda_init)
    return pl.pallas_call(
        kern,
        grid=(bsz, DIFF_HEADS, nq),
        in_specs=[pl.BlockSpec((4, DIFF_QK_DIM), lambda b, h, i: (0, 0)),
                  pl.BlockSpec((1, tq, dv), lambda b, h, i: (b, i, h)),
                  pl.BlockSpec((1, seq, dv), lambda b, h, i: (b, 0, DIFF_HEADS + h)),
                  pl.BlockSpec((1, seq, dv), lambda b, h, i: (b, 0, 2 * DIFF_HEADS + h)),
                  pl.BlockSpec((dv, 1), lambda b, h, i: (0, 0))],
        out_specs=pl.BlockSpec((1, tq, dv), lambda b, h, i: (b, i, h)),
        out_shape=jax.ShapeDtypeStruct((bsz, seq, DIFF_WIDTH), BF16),
        scratch_shapes=[pltpu.VMEM((dv + BF16_ROWS, seq), BF16),
                        pltpu.VMEM((2, 2, tk, tq), F32),
                        pltpu.VMEM((2, 2, tk, tq), BF16),
                        pltpu.VMEM((2, 2, 1, tq), F32),
                        pltpu.VMEM((2, 1, tq), F32),
                        pltpu.VMEM((2, dv + BF16_ROWS, tq), F32)],
        compiler_params=_params(3),
        name="diff_attention",
    )(lam_rows, qkvx, qkvx, qkvx, g_sub.reshape(dv, 1))


def _cross_attn_kernel(xq_ref, kv_ref, o_ref):
    hd = CROSS_HEAD_DIM
    for h in range(CROSS_HEADS):
        q = xq_ref[0, :, h * hd:(h + 1) * hd]
        k = kv_ref[0, :, h * hd:(h + 1) * hd]
        v = kv_ref[0, :, CROSS_WIDTH + h * hd:CROSS_WIDTH + (h + 1) * hd]
        s = _dot_nt(q, k) * (hd ** -0.5)
        p = jnp.exp(s - jnp.max(s, axis=-1, keepdims=True))
        l = jnp.sum(p, axis=-1, keepdims=True)
        o_ref[0, :, h * hd:(h + 1) * hd] = (_dot(p.astype(BF16), v) / l).astype(o_ref.dtype)


def cross_attention(qkvx, kv, bsz, seq):
    tm = _tile(seq, 512)
    mem_len = kv.shape[1]
    return pl.pallas_call(
        _cross_attn_kernel,
        grid=(bsz, seq // tm),
        in_specs=[pl.BlockSpec((1, tm, CROSS_WIDTH), lambda b, i: (b, i, 3)),
                  pl.BlockSpec((1, mem_len, 2 * CROSS_WIDTH), lambda b, i: (b, 0, 0))],
        out_specs=pl.BlockSpec((1, tm, CROSS_WIDTH), lambda b, i: (b, i, 0)),
        out_shape=jax.ShapeDtypeStruct((bsz, seq, CROSS_WIDTH), BF16),
        compiler_params=_params(2),
        name="cross_attention",
    )(qkvx, kv)


def _merge_kernel(h_ref, ya_ref, yd_ref, yx_ref, wga_ref, wgb_ref, wgc_ref,
                  ba_ref, bb_ref, bc_ref, wa_ref, wd_ref, wx_ref, o_ref):
    h = h_ref[...]
    merged = jax.nn.sigmoid(_dot(h, wga_ref[...]) + ba_ref[...]) * _dot(ya_ref[...], wa_ref[...])
    merged += jax.nn.sigmoid(_dot(h, wgb_ref[...]) + bb_ref[...]) * _dot(yd_ref[...], wd_ref[...])
    merged += jax.nn.sigmoid(_dot(h, wgc_ref[...]) + bc_ref[...]) * _dot(yx_ref[...], wx_ref[...])
    o_ref[...] = merged.astype(o_ref.dtype)


def gated_merge(h, ya, yd, yx, w_gate, b_gate, w_a, w_d, w_x):
    m, d = h.shape
    tm = _tile(m, 1024)
    tn = 512
    nj = d // tn
    act = lambda width: pl.BlockSpec((tm, width), lambda i, j: (i, 0))
    gate_w = lambda br: pl.BlockSpec((d, tn), lambda i, j: (0, br * nj + j))
    gate_b = lambda br: pl.BlockSpec((1, tn), lambda i, j: (0, br * nj + j))
    out_w = lambda width: pl.BlockSpec((width, tn), lambda i, j: (0, j))
    return pl.pallas_call(
        _merge_kernel,
        grid=(m // tm, nj),
        in_specs=[act(d), act(CONV_WIDTH), act(DIFF_WIDTH), act(CROSS_WIDTH),
                  gate_w(0), gate_w(1), gate_w(2), gate_b(0), gate_b(1), gate_b(2),
                  out_w(CONV_WIDTH), out_w(DIFF_WIDTH), out_w(CROSS_WIDTH)],
        out_specs=pl.BlockSpec((tm, tn), lambda i, j: (i, j)),
        out_shape=jax.ShapeDtypeStruct((m, d), BF16),
        compiler_params=_params(2),
        name="gated_merge",
    )(h, ya, yd, yx, w_gate, w_gate, w_gate, b_gate, b_gate, b_gate, w_a, w_d, w_x)


def _out_proj_kernel(z_ref, w_ref, x_ref, gpost_ref, gpre_ref, x1_ref, h2_ref):
    y = _dot(z_ref[...], w_ref[...])
    x1 = x_ref[...] + _rmsnorm_rows(y, gpost_ref[...])
    x1_ref[...] = x1
    h2_ref[...] = _rmsnorm_rows(x1, gpre_ref[...]).astype(h2_ref.dtype)


def out_proj_residual(z, w_o, x, g_post, g_next_pre):
    m, d = x.shape
    tm = _tile(m, 512)
    row = lambda: pl.BlockSpec((tm, d), lambda i: (i, 0))
    vec = lambda: pl.BlockSpec((1, d), lambda i: (0, 0))
    return pl.pallas_call(
        _out_proj_kernel,
        grid=(m // tm,),
        in_specs=[row(), pl.BlockSpec((d, d), lambda i: (0, 0)), row(), vec(), vec()],
        out_specs=[row(), row()],
        out_shape=[jax.ShapeDtypeStruct((m, d), F32), jax.ShapeDtypeStruct((m, d), BF16)],
        compiler_params=_params(1),
        name="out_proj_residual",
    )(z, w_o, x, g_post.reshape(1, d), g_next_pre.reshape(1, d))


def _ffn_up_kernel(h_ref, wg_ref, wv_ref, cwg_ref, cwv_ref, o_ref, carry_g, carry_v, *, tiles_per_seq):
    i = pl.program_id(0)
    j = pl.program_id(1)
    h = h_ref[...]
    start = i % tiles_per_seq == 0
    halves = []
    for w_ref, cw_ref, carry_ref in ((wg_ref, cwg_ref, carry_g), (wv_ref, cwv_ref, carry_v)):
        z = _dot(h, w_ref[...])
        prev = _load_carry(carry_ref, j, start)
        halves.append(_causal_conv3(z, prev, cw_ref[...]))
        carry_ref[j] = z[z.shape[0] - SUBLANES:, :]
    gate, val = halves
    o_ref[...] = (gate * jax.nn.sigmoid(gate) * val).astype(o_ref.dtype)


def ffn_up(h, w_up, conv_w, seq):
    m, d = h.shape
    d_ff = w_up.shape[1] // 2
    tm = _tile(seq, 1024)
    tn = 512
    nj = d_ff // tn
    kern = functools.partial(_ffn_up_kernel, tiles_per_seq=seq // tm)
    return pl.pallas_call(
        kern,
        grid=(m // tm, nj),
        in_specs=[pl.BlockSpec((tm, d), lambda i, j: (i, 0)),
                  pl.BlockSpec((d, tn), lambda i, j: (0, j)),
                  pl.BlockSpec((d, tn), lambda i, j: (0, nj + j)),
                  pl.BlockSpec((CONV_K, tn), lambda i, j: (0, j)),
                  pl.BlockSpec((CONV_K, tn), lambda i, j: (0, nj + j))],
        out_specs=pl.BlockSpec((tm, tn), lambda i, j: (i, j)),
        out_shape=jax.ShapeDtypeStruct((m, d_ff), BF16),
        scratch_shapes=[pltpu.VMEM((nj, SUBLANES, tn), F32), pltpu.VMEM((nj, SUBLANES, tn), F32)],
        compiler_params=_params(2),
        name="ffn_up",
    )(h, w_up, w_up, conv_w, conv_w)


def _ffn_down_kernel(a_ref, w_ref, x_ref, g_ref, o_ref, acc_ref):
    k = pl.program_id(1)

    @pl.when(k == 0)
    def _():
        acc_ref[...] = jnp.zeros(acc_ref.shape, F32)

    acc_ref[...] += _dot(a_ref[...], w_ref[...])

    @pl.when(k == pl.num_programs(1) - 1)
    def _():
        o_ref[...] = x_ref[...] + _rmsnorm_rows(acc_ref[...], g_ref[...])


def ffn_down(a, w_down, x, g_post):
    m, d = x.shape
    d_ff = a.shape[1]
    tm = _tile(m, 512)
    tk = 1408
    row = lambda: pl.BlockSpec((tm, d), lambda i, k: (i, 0))
    return pl.pallas_call(
        _ffn_down_kernel,
        grid=(m // tm, d_ff // tk),
        in_specs=[pl.BlockSpec((tm, tk), lambda i, k: (i, k)),
                  pl.BlockSpec((tk, d), lambda i, k: (k, 0)),
                  row(), pl.BlockSpec((1, d), lambda i, k: (0, 0))],
        out_specs=row(),
        out_shape=jax.ShapeDtypeStruct((m, d), F32),
        scratch_shapes=[pltpu.VMEM((tm, d), F32)],
        compiler_params=_params(2),
        name="ffn_down",
    )(a, w_down, x, g_post.reshape(1, d))


def kernel(x, mem, g_mix_pre, w_in, w_gate, b_gate, conv_w, w_conv_out, lambda_q1, lambda_k1, lambda_q2, lambda_k2, g_diff_sub, w_diff_out, g_mem, w_mem_kv, w_cross_out, w_o, g_mix_post, g_ffn_pre, w_up, ffn_conv_w, w_down, g_ffn_post):
    bsz, seq, d = x.shape
    mem_len = mem.shape[1]
    depth = w_in.shape[0]
    m = bsz * seq
    q_lo = N_BRANCH * CONV_WIDTH
    qkvx_width = 2 * DIFF_QK_WIDTH + DIFF_WIDTH + CROSS_WIDTH
    qkvx_scale = jnp.where(jnp.arange(qkvx_width) < DIFF_QK_WIDTH,
                           LOG2E * DIFF_QK_DIM ** -0.5, 1.0).astype(F32)
    kv_scale = jnp.ones((2 * CROSS_WIDTH,), F32)

    xf = x.reshape(m, d)
    h = rmsnorm_bf16(xf, g_mix_pre[0])
    for l in range(depth):
        lambda_init = 0.8 - 0.6 * math.exp(-0.3 * l)
        w_in_b = w_in[l].astype(BF16)
        ya = conv_mixer(h, w_in_b, conv_w[l], seq)
        qkvx = project(h, w_in_b, qkvx_scale, q_lo // 1024).reshape(bsz, seq, qkvx_width)

        lam_rows = jnp.stack([lambda_q1[l], lambda_k1[l], lambda_q2[l], lambda_k2[l]])
        yd = diff_attention(qkvx, lam_rows, g_diff_sub[l], bsz, seq, lambda_init)

        mem_n = rmsnorm_bf16(mem.reshape(bsz * mem_len, d), g_mem[l])
        kv = project(mem_n, w_mem_kv[l].astype(BF16), kv_scale, 0, tm_want=bsz * mem_len)
        yx = cross_attention(qkvx, kv.reshape(bsz, mem_len, 2 * CROSS_WIDTH), bsz, seq)

        merged = gated_merge(h, ya, yd.reshape(m, DIFF_WIDTH), yx.reshape(m, CROSS_WIDTH),
                             w_gate[l].astype(BF16), b_gate[l].reshape(1, -1),
                             w_conv_out[l].astype(BF16), w_diff_out[l].astype(BF16),
                             w_cross_out[l].astype(BF16))
        xf, h2 = out_proj_residual(merged, w_o[l].astype(BF16), xf, g_mix_post[l], g_ffn_pre[l])

        act = ffn_up(h2, w_up[l].astype(BF16), ffn_conv_w[l], seq)
        xf = ffn_down(act, w_down[l].astype(BF16), xf, g_ffn_post[l])
        if l + 1 < depth:
            h = rmsnorm_bf16(xf, g_mix_pre[l + 1])
    return xf.reshape(bsz, seq, d)
```

```python
import functools
import math

import jax
import jax.numpy as jnp
from jax import lax
from jax.experimental import pallas as pl
from jax.experimental.pallas import tpu as pltpu

EPS = 1e-6
CHUNK = 64
CONV_WIDTH = 1024
CONV_K = 3
DIFF_HEADS = 8
DIFF_QK_DIM = 64
DIFF_V_DIM = 2 * DIFF_QK_DIM
DIFF_QK_WIDTH = DIFF_HEADS * 2 * DIFF_QK_DIM
DIFF_WIDTH = DIFF_HEADS * DIFF_V_DIM
CROSS_HEADS = 4
CROSS_HEAD_DIM = 256
CROSS_WIDTH = CROSS_HEADS * CROSS_HEAD_DIM
N_BRANCH = 3
LOG2E = math.log2(math.e)
MAX_EXCESS = 30.0

SUBLANES = 8
BF16_ROWS = 16
EPILOGUE_ROWS = 256
VMEM_LIMIT = 56 * 1024 * 1024

F32 = jnp.float32
BF16 = jnp.bfloat16


def _params(n_axes):
    return pltpu.CompilerParams(
        dimension_semantics=("arbitrary",) * n_axes, vmem_limit_bytes=VMEM_LIMIT)


def _dot(a, b):
    return jnp.dot(a, b, preferred_element_type=F32)


def _dot_nt(a, b):
    return lax.dot_general(a, b, (((1,), (1,)), ((), ())), preferred_element_type=F32)


def _tile(n, want):
    t = min(n, want)
    assert n % t == 0, (n, want)
    return t


def _rmsnorm_rows(x, g):
    ms = jnp.mean(x * x, axis=-1, keepdims=True)
    return x * lax.rsqrt(ms + EPS) * g


def _rmsnorm_kernel(x_ref, g_ref, o_ref):
    o_ref[...] = _rmsnorm_rows(x_ref[...], g_ref[...]).astype(o_ref.dtype)


def rmsnorm_bf16(x, g):
    m, d = x.shape
    tm = _tile(m, 512)
    return pl.pallas_call(
        _rmsnorm_kernel,
        grid=(m // tm,),
        in_specs=[pl.BlockSpec((tm, d), lambda i: (i, 0)),
                  pl.BlockSpec((1, d), lambda i: (0, 0))],
        out_specs=pl.BlockSpec((tm, d), lambda i: (i, 0)),
        out_shape=jax.ShapeDtypeStruct((m, d), BF16),
        compiler_params=_params(1),
        name="rmsnorm",
    )(x, g.reshape(1, d))


def _shift_rows(u, prev, s):
    rolled = pltpu.roll(u, s, axis=0)
    rprev = pltpu.roll(prev, s, axis=0)
    rows = lax.broadcasted_iota(jnp.int32, prev.shape, 0)
    first = jnp.where(rows < s, rprev, rolled[0:SUBLANES])
    return jnp.concatenate([first, rolled[SUBLANES:]], axis=0)


def _causal_conv3(u, prev, w):
    return (w[2:3] * u + w[1:2] * _shift_rows(u, prev, 1) + w[0:1] * _shift_rows(u, prev, 2))


def _load_carry(carry_ref, j, at_seq_start):
    @pl.when(at_seq_start)
    def _():
        carry_ref[j] = jnp.zeros(carry_ref.shape[1:], carry_ref.dtype)
    return carry_ref[j]


def _conv_mixer_kernel(h_ref, wb_ref, wc_ref, wv_ref, cw_ref, o_ref, carry_ref, *, tiles_per_seq):
    i = pl.program_id(0)
    j = pl.program_id(1)
    h = h_ref[...]
    u = _dot(h, wc_ref[...]) * _dot(h, wv_ref[...])
    prev = _load_carry(carry_ref, j, i % tiles_per_seq == 0)
    conv = _causal_conv3(u, prev, cw_ref[...])
    carry_ref[j] = u[u.shape[0] - SUBLANES:, :]
    o_ref[...] = (_dot(h, wb_ref[...]) * conv).astype(o_ref.dtype)


def conv_mixer(h, w_in, conv_w, seq):
    m, d = h.shape
    tm = _tile(seq, 1024)
    tn = 512
    nj = CONV_WIDTH // tn
    kern = functools.partial(_conv_mixer_kernel, tiles_per_seq=seq // tm)
    return pl.pallas_call(
        kern,
        grid=(m // tm, nj),
        in_specs=[pl.BlockSpec((tm, d), lambda i, j: (i, 0)),
                  pl.BlockSpec((d, tn), lambda i, j: (0, j)),
                  pl.BlockSpec((d, tn), lambda i, j: (0, nj + j)),
                  pl.BlockSpec((d, tn), lambda i, j: (0, 2 * nj + j)),
                  pl.BlockSpec((CONV_K, tn), lambda i, j: (0, j))],
        out_specs=pl.BlockSpec((tm, tn), lambda i, j: (i, j)),
        out_shape=jax.ShapeDtypeStruct((m, CONV_WIDTH), BF16),
        scratch_shapes=[pltpu.VMEM((nj, SUBLANES, tn), F32)],
        compiler_params=_params(2),
        name="conv_mixer",
    )(h, w_in, w_in, w_in, conv_w)


def _proj_kernel(h_ref, w_ref, scale_ref, o_ref):
    o_ref[...] = (_dot(h_ref[...], w_ref[...]) * scale_ref[...]).astype(o_ref.dtype)


def project(h, w, col_scale, col_block0, tm_want=1024, tn=1024):
    m, d = h.shape
    n_out = col_scale.shape[0]
    tm = _tile(m, tm_want)
    return pl.pallas_call(
        _proj_kernel,
        grid=(m // tm, n_out // tn),
        in_specs=[pl.BlockSpec((tm, d), lambda i, j: (i, 0)),
                  pl.BlockSpec((d, tn), lambda i, j: (0, col_block0 + j)),
                  pl.BlockSpec((1, tn), lambda i, j: (0, j))],
        out_specs=pl.BlockSpec((tm, tn), lambda i, j: (i, j)),
        out_shape=jax.ShapeDtypeStruct((m, n_out), BF16),
        compiler_params=_params(2),
        name="project",
    )(h, w, col_scale.reshape(1, n_out))


def _col_max(s):
    r = s.shape[0]
    while r > SUBLANES:
        r //= 2
        s = jnp.maximum(s[:r], s[r:])
    return jnp.max(s, axis=0, keepdims=True)


def _diff_attn_kernel(lam_ref, q_ref, k_ref, v_ref, gsub_ref, o_ref,
                      vt_ref, p_ref, alpha_ref, m_ref, excess_ref, acc_ref, *, tq, tk, lambda_init):
    qi = pl.program_id(2)
    seq = k_ref.shape[1]
    dv = DIFF_V_DIM
    assert tq == 2 * tk

    @pl.when(qi == 0)
    def _():
        vt_ref[dv:, :] = jnp.ones((BF16_ROWS, seq), BF16)

        def body(c, carry):
            r = pl.multiple_of(c * tq, tq)
            vt_ref[0:dv, pl.ds(r, tq)] = v_ref[0, pl.ds(r, tq), :].astype(F32).T.astype(BF16)
            return carry
        lax.fori_loop(0, seq // tq, body, 0)

    q = q_ref[0].astype(F32)
    lane = lax.broadcasted_iota(jnp.int32, q.shape, 1)
    qz = (jnp.where(lane < DIFF_QK_DIM, q, 0.0).astype(BF16),
          jnp.where(lane >= DIFF_QK_DIM, q, 0.0).astype(BF16))

    def k_tile(t):
        return k_ref[0, pl.ds(pl.multiple_of(t * tk, tk), tk), :]

    def vt_tile(t):
        return vt_ref[:, pl.ds(pl.multiple_of(t * tk, tk), tk)]

    key_chunk = lax.broadcasted_iota(jnp.int32, (tk, tq), 0) // CHUNK
    qry_chunk = lax.broadcasted_iota(jnp.int32, (tk, tq), 1) // CHUNK
    diag_masks = (key_chunk <= qry_chunk, key_chunk + tk // CHUNK <= qry_chunk)

    def exact_tile(t, mask):
        k_t = k_tile(t)
        vt_t = vt_tile(t)
        for c in range(2):
            s = _dot_nt(k_t, qz[c])
            if mask is not None:
                s = jnp.where(mask, s, -jnp.inf)
            m_old = m_ref[c]
            m_new = jnp.maximum(m_old, _col_max(s))
            p = jnp.exp2(s - m_new).astype(BF16)
            acc_ref[c] = jnp.exp2(m_old - m_new) * acc_ref[c] + _dot(vt_t, p)
            m_ref[c] = m_new

    def exact_diagonal():
        acc_ref[...] = jnp.zeros(acc_ref.shape, F32)
        m_ref[...] = jnp.full(m_ref.shape, -jnp.inf, F32)
        for d in range(2):
            exact_tile(2 * qi + d, diag_masks[d])

    def fast_pair(j, carry):
        t_a = 2 * j
        k_a = k_tile(t_a)
        k_b = k_tile(t_a + 1)
        vt_prev = vt_tile(jnp.maximum(t_a - 1, 0))
        vt_a = vt_tile(t_a)
        m_frozen = [m_ref[c] for c in range(2)]
        tile_max = []
        for c in range(2):
            s = _dot_nt(k_a, qz[c])
            tile_max.append(_col_max(s))
            p_ref[0, c] = jnp.exp2(s - m_frozen[c]).astype(BF16)
        for c in range(2):
            pending = acc_ref[c] + _dot(vt_prev, p_ref[1, c])
            acc_ref[c] = alpha_ref[c] * pending + _dot(vt_a, p_ref[0, c])
        for c in range(2):
            s = _dot_nt(k_b, qz[c])
            tile_max[c] = jnp.maximum(tile_max[c], _col_max(s))
            p_ref[1, c] = jnp.exp2(s - m_frozen[c]).astype(BF16)
        for c in range(2):
            m_new = jnp.maximum(m_frozen[c], tile_max[c])
            alpha_ref[c] = jnp.exp2(m_frozen[c] - m_new)
            excess_ref[c] = jnp.maximum(excess_ref[c], tile_max[c] - m_frozen[c])
            m_ref[c] = m_new
        return carry

    exact_diagonal()
    p_ref[1] = jnp.zeros(p_ref.shape[1:], BF16)
    alpha_ref[...] = jnp.ones(alpha_ref.shape, F32)
    excess_ref[...] = jnp.zeros(excess_ref.shape, F32)
    lax.fori_loop(0, qi, fast_pair, 0)
    vt_last = vt_tile(jnp.maximum(2 * qi - 1, 0))
    for c in range(2):
        acc_ref[c] = alpha_ref[c] * (acc_ref[c] + _dot(vt_last, p_ref[1, c]))

    worst = jnp.max(jnp.maximum(excess_ref[0], excess_ref[1]), axis=-1, keepdims=True)

    @pl.when(worst[0, 0] > MAX_EXCESS)
    def _():
        exact_diagonal()

        def body(t, carry):
            exact_tile(t, None)
            return carry
        lax.fori_loop(0, 2 * qi, body, 0)

    lv = lam_ref[...]
    lam = (jnp.exp(jnp.sum(lv[0:1] * lv[1:2], axis=-1, keepdims=True))
           - jnp.exp(jnp.sum(lv[2:3] * lv[3:4], axis=-1, keepdims=True)) + lambda_init)
    a1 = acc_ref[0]
    a2 = acc_ref[1]
    o = a1[0:dv] / a1[dv:dv + 1] - lam * (a2[0:dv] / a2[dv:dv + 1])
    ms = jnp.mean(o * o, axis=0, keepdims=True)
    y = o * lax.rsqrt(ms + EPS) * gsub_ref[...]
    y = y * (1.0 - lambda_init)
    o_ref[0] = y.T.astype(o_ref.dtype)


def diff_attention(qkvx, lam_rows, g_sub, bsz, seq, lambda_init):
    tq = _tile(seq, 512)
    tk = tq // 2
    nq = seq // tq
    dv = DIFF_V_DIM
    kern = functools.partial(_diff_attn_kernel, tq=tq, tk=tk, lambda_init=lambda_init)
    return pl.pallas_call(
        kern,
        grid=(bsz, DIFF_HEADS, nq),
        in_specs=[pl.BlockSpec((4, DIFF_QK_DIM), lambda b, h, i: (0, 0)),
                  pl.BlockSpec((1, tq, dv), lambda b, h, i: (b, i, h)),
                  pl.BlockSpec((1, seq, dv), lambda b, h, i: (b, 0, DIFF_HEADS + h)),
                  pl.BlockSpec((1, seq, dv), lambda b, h, i: (b, 0, 2 * DIFF_HEADS + h)),
                  pl.BlockSpec((dv, 1), lambda b, h, i: (0, 0))],
        out_specs=pl.BlockSpec((1, tq, dv), lambda b, h, i: (b, i, h)),
        out_shape=jax.ShapeDtypeStruct((bsz, seq, DIFF_WIDTH), BF16),
        scratch_shapes=[pltpu.VMEM((dv + BF16_ROWS, seq), BF16),
                        pltpu.VMEM((2, 2, tk, tq), BF16),
                        pltpu.VMEM((2, 1, tq), F32),
                        pltpu.VMEM((2, 1, tq), F32),
                        pltpu.VMEM((2, 1, tq), F32),
                        pltpu.VMEM((2, dv + BF16_ROWS, tq), F32)],
        compiler_params=_params(3),
        name="diff_attention",
    )(lam_rows, qkvx, qkvx, qkvx, g_sub.reshape(dv, 1))


def _cross_attn_kernel(xq_ref, kv_ref, o_ref):
    hd = CROSS_HEAD_DIM
    for h in range(CROSS_HEADS):
        q = xq_ref[0, :, h * hd:(h + 1) * hd]
        k = kv_ref[0, :, h * hd:(h + 1) * hd]
        v = kv_ref[0, :, CROSS_WIDTH + h * hd:CROSS_WIDTH + (h + 1) * hd]
        s = _dot_nt(q, k) * (hd ** -0.5)
        p = jnp.exp(s - jnp.max(s, axis=-1, keepdims=True))
        l = jnp.sum(p, axis=-1, keepdims=True)
        o_ref[0, :, h * hd:(h + 1) * hd] = (_dot(p.astype(BF16), v) / l).astype(o_ref.dtype)


def cross_attention(qkvx, kv, bsz, seq):
    tm = _tile(seq, 512)
    mem_len = kv.shape[1]
    return pl.pallas_call(
        _cross_attn_kernel,
        grid=(bsz, seq // tm),
        in_specs=[pl.BlockSpec((1, tm, CROSS_WIDTH), lambda b, i: (b, i, 3)),
                  pl.BlockSpec((1, mem_len, 2 * CROSS_WIDTH), lambda b, i: (b, 0, 0))],
        out_specs=pl.BlockSpec((1, tm, CROSS_WIDTH), lambda b, i: (b, i, 0)),
        out_shape=jax.ShapeDtypeStruct((bsz, seq, CROSS_WIDTH), BF16),
        compiler_params=_params(2),
        name="cross_attention",
    )(qkvx, kv)


def _merge_kernel(h_ref, ya_ref, yd_ref, yx_ref, wga_ref, wgb_ref, wgc_ref,
                  ba_ref, bb_ref, bc_ref, wa_ref, wd_ref, wx_ref, o_ref):
    h = h_ref[...]
    merged = jax.nn.sigmoid(_dot(h, wga_ref[...]) + ba_ref[...]) * _dot(ya_ref[...], wa_ref[...])
    merged += jax.nn.sigmoid(_dot(h, wgb_ref[...]) + bb_ref[...]) * _dot(yd_ref[...], wd_ref[...])
    merged += jax.nn.sigmoid(_dot(h, wgc_ref[...]) + bc_ref[...]) * _dot(yx_ref[...], wx_ref[...])
    o_ref[...] = merged.astype(o_ref.dtype)


def gated_merge(h, ya, yd, yx, w_gate, b_gate, w_a, w_d, w_x):
    m, d = h.shape
    tm = _tile(m, 1024)
    tn = 512
    nj = d // tn
    act = lambda width: pl.BlockSpec((tm, width), lambda i, j: (i, 0))
    gate_w = lambda br: pl.BlockSpec((d, tn), lambda i, j: (0, br * nj + j))
    gate_b = lambda br: pl.BlockSpec((1, tn), lambda i, j: (0, br * nj + j))
    out_w = lambda width: pl.BlockSpec((width, tn), lambda i, j: (0, j))
    return pl.pallas_call(
        _merge_kernel,
        grid=(m // tm, nj),
        in_specs=[act(d), act(CONV_WIDTH), act(DIFF_WIDTH), act(CROSS_WIDTH),
                  gate_w(0), gate_w(1), gate_w(2), gate_b(0), gate_b(1), gate_b(2),
                  out_w(CONV_WIDTH), out_w(DIFF_WIDTH), out_w(CROSS_WIDTH)],
        out_specs=pl.BlockSpec((tm, tn), lambda i, j: (i, j)),
        out_shape=jax.ShapeDtypeStruct((m, d), BF16),
        compiler_params=_params(2),
        name="gated_merge",
    )(h, ya, yd, yx, w_gate, w_gate, w_gate, b_gate, b_gate, b_gate, w_a, w_d, w_x)


def _out_proj_kernel(z_ref, w_ref, x_ref, gpost_ref, gpre_ref, x1_ref, h2_ref):
    for r0 in range(0, z_ref.shape[0], EPILOGUE_ROWS):
        rows = slice(r0, r0 + EPILOGUE_ROWS)
        y = _dot(z_ref[rows, :], w_ref[...])
        x1 = x_ref[rows, :] + _rmsnorm_rows(y, gpost_ref[...])
        x1_ref[rows, :] = x1
        h2_ref[rows, :] = _rmsnorm_rows(x1, gpre_ref[...]).astype(h2_ref.dtype)


def out_proj_residual(z, w_o, x, g_post, g_next_pre):
    m, d = x.shape
    tm = _tile(m, 512)
    row = lambda: pl.BlockSpec((tm, d), lambda i: (i, 0))
    vec = lambda: pl.BlockSpec((1, d), lambda i: (0, 0))
    return pl.pallas_call(
        _out_proj_kernel,
        grid=(m // tm,),
        in_specs=[row(), pl.BlockSpec((d, d), lambda i: (0, 0)), row(), vec(), vec()],
        out_specs=[row(), row()],
        out_shape=[jax.ShapeDtypeStruct((m, d), F32), jax.ShapeDtypeStruct((m, d), BF16)],
        compiler_params=_params(1),
        name="out_proj_residual",
    )(z, w_o, x, g_post.reshape(1, d), g_next_pre.reshape(1, d))


def _ffn_up_kernel(h_ref, wg_ref, wv_ref, cwg_ref, cwv_ref, o_ref, carry_g, carry_v, *, tiles_per_seq):
    i = pl.program_id(0)
    j = pl.program_id(1)
    start = i % tiles_per_seq == 0
    prev_g = _load_carry(carry_g, j, start)
    prev_v = _load_carry(carry_v, j, start)
    for r0 in range(0, h_ref.shape[0], EPILOGUE_ROWS):
        h = h_ref[r0:r0 + EPILOGUE_ROWS, :]
        zg = _dot(h, wg_ref[...])
        zv = _dot(h, wv_ref[...])
        gate = _causal_conv3(zg, prev_g, cwg_ref[...])
        val = _causal_conv3(zv, prev_v, cwv_ref[...])
        o_ref[r0:r0 + EPILOGUE_ROWS, :] = (gate * jax.nn.sigmoid(gate) * val).astype(o_ref.dtype)
        prev_g = zg[EPILOGUE_ROWS - SUBLANES:, :]
        prev_v = zv[EPILOGUE_ROWS - SUBLANES:, :]
    carry_g[j] = prev_g
    carry_v[j] = prev_v


def ffn_up(h, w_up, conv_w, seq):
    m, d = h.shape
    d_ff = w_up.shape[1] // 2
    tm = _tile(seq, 1024)
    tn = 512
    nj = d_ff // tn
    kern = functools.partial(_ffn_up_kernel, tiles_per_seq=seq // tm)
    return pl.pallas_call(
        kern,
        grid=(m // tm, nj),
        in_specs=[pl.BlockSpec((tm, d), lambda i, j: (i, 0)),
                  pl.BlockSpec((d, tn), lambda i, j: (0, j)),
                  pl.BlockSpec((d, tn), lambda i, j: (0, nj + j)),
                  pl.BlockSpec((CONV_K, tn), lambda i, j: (0, j)),
                  pl.BlockSpec((CONV_K, tn), lambda i, j: (0, nj + j))],
        out_specs=pl.BlockSpec((tm, tn), lambda i, j: (i, j)),
        out_shape=jax.ShapeDtypeStruct((m, d_ff), BF16),
        scratch_shapes=[pltpu.VMEM((nj, SUBLANES, tn), F32), pltpu.VMEM((nj, SUBLANES, tn), F32)],
        compiler_params=_params(2),
        name="ffn_up",
    )(h, w_up, w_up, conv_w, conv_w)


def _ffn_down_kernel(a_ref, w_ref, x_ref, g_ref, o_ref, acc_ref):
    k = pl.program_id(1)
    last = pl.num_programs(1) - 1

    @pl.when(k == 0)
    def _():
        acc_ref[...] = _dot(a_ref[...], w_ref[...])

    @pl.when((k > 0) & (k < last))
    def _():
        acc_ref[...] += _dot(a_ref[...], w_ref[...])

    @pl.when(k == last)
    def _():
        for r0 in range(0, a_ref.shape[0], EPILOGUE_ROWS):
            rows = slice(r0, r0 + EPILOGUE_ROWS)
            y = acc_ref[rows, :] + _dot(a_ref[rows, :], w_ref[...])
            o_ref[rows, :] = x_ref[rows, :] + _rmsnorm_rows(y, g_ref[...])


def ffn_down(a, w_down, x, g_post):
    m, d = x.shape
    d_ff = a.shape[1]
    tm = _tile(m, 1024)
    tk = 512
    assert d_ff // tk >= 2
    row = lambda: pl.BlockSpec((tm, d), lambda i, k: (i, 0))
    return pl.pallas_call(
        _ffn_down_kernel,
        grid=(m // tm, d_ff // tk),
        in_specs=[pl.BlockSpec((tm, tk), lambda i, k: (i, k)),
                  pl.BlockSpec((tk, d), lambda i, k: (k, 0)),
                  row(), pl.BlockSpec((1, d), lambda i, k: (0, 0))],
        out_specs=row(),
        out_shape=jax.ShapeDtypeStruct((m, d), F32),
        scratch_shapes=[pltpu.VMEM((tm, d), F32)],
        compiler_params=_params(2),
        name="ffn_down",
    )(a, w_down, x, g_post.reshape(1, d))


def kernel(x, mem, g_mix_pre, w_in, w_gate, b_gate, conv_w, w_conv_out, lambda_q1, lambda_k1, lambda_q2, lambda_k2, g_diff_sub, w_diff_out, g_mem, w_mem_kv, w_cross_out, w_o, g_mix_post, g_ffn_pre, w_up, ffn_conv_w, w_down, g_ffn_post):
    bsz, seq, d = x.shape
    mem_len = mem.shape[1]
    depth = w_in.shape[0]
    m = bsz * seq
    q_lo = N_BRANCH * CONV_WIDTH
    qkvx_width = 2 * DIFF_QK_WIDTH + DIFF_WIDTH + CROSS_WIDTH
    qkvx_scale = jnp.where(jnp.arange(qkvx_width) < DIFF_QK_WIDTH,
                           LOG2E * DIFF_QK_DIM ** -0.5, 1.0).astype(F32)
    kv_scale = jnp.ones((2 * CROSS_WIDTH,), F32)

    xf = x.reshape(m, d)
    h = rmsnorm_bf16(xf, g_mix_pre[0])
    for l in range(depth):
        lambda_init = 0.8 - 0.6 * math.exp(-0.3 * l)
        w_in_b = w_in[l].astype(BF16)
        ya = conv_mixer(h, w_in_b, conv_w[l], seq)
        qkvx = project(h, w_in_b, qkvx_scale, q_lo // 1024).reshape(bsz, seq, qkvx_width)

        lam_rows = jnp.stack([lambda_q1[l], lambda_k1[l], lambda_q2[l], lambda_k2[l]])
        yd = diff_attention(qkvx, lam_rows, g_diff_sub[l], bsz, seq, lambda_init)

        mem_n = rmsnorm_bf16(mem.reshape(bsz * mem_len, d), g_mem[l])
        kv = project(mem_n, w_mem_kv[l].astype(BF16), kv_scale, 0, tm_want=bsz * mem_len)
        yx = cross_attention(qkvx, kv.reshape(bsz, mem_len, 2 * CROSS_WIDTH), bsz, seq)

        merged = gated_merge(h, ya, yd.reshape(m, DIFF_WIDTH), yx.reshape(m, CROSS_WIDTH),
                             w_gate[l].astype(BF16), b_gate[l].reshape(1, -1),
                             w_conv_out[l].astype(BF16), w_diff_out[l].astype(BF16),
                             w_cross_out[l].astype(BF16))
        xf, h2 = out_proj_residual(merged, w_o[l].astype(BF16), xf, g_mix_post[l], g_ffn_pre[l])

        act = ffn_up(h2, w_up[l].astype(BF16), ffn_conv_w[l], seq)
        xf = ffn_down(act, w_down[l].astype(BF16), xf, g_ffn_post[l])
        if l + 1 < depth:
            h = rmsnorm_bf16(xf, g_mix_pre[l + 1])
    return xf.reshape(bsz, seq, d)
```

```python
import functools
import math

import jax
import jax.numpy as jnp
from jax import lax
from jax.experimental import pallas as pl
from jax.experimental.pallas import tpu as pltpu

EPS = 1e-6
CHUNK = 64
CONV_WIDTH = 1024
CONV_K = 3
DIFF_HEADS = 8
DIFF_QK_DIM = 64
DIFF_V_DIM = 2 * DIFF_QK_DIM
DIFF_QK_WIDTH = DIFF_HEADS * 2 * DIFF_QK_DIM
DIFF_WIDTH = DIFF_HEADS * DIFF_V_DIM
CROSS_HEADS = 4
CROSS_HEAD_DIM = 256
CROSS_WIDTH = CROSS_HEADS * CROSS_HEAD_DIM
N_BRANCH = 3
LOG2E = math.log2(math.e)
MAX_EXCESS = 30.0
GROUP = 4

SUBLANES = 8
BF16_ROWS = 16
EPILOGUE_ROWS = 256
VMEM_LIMIT = 56 * 1024 * 1024

F32 = jnp.float32
BF16 = jnp.bfloat16


def _params(n_axes):
    return pltpu.CompilerParams(
        dimension_semantics=("arbitrary",) * n_axes, vmem_limit_bytes=VMEM_LIMIT)


def _dot(a, b):
    return jnp.dot(a, b, preferred_element_type=F32)


def _dot_nt(a, b):
    return lax.dot_general(a, b, (((1,), (1,)), ((), ())), preferred_element_type=F32)


def _tile(n, want):
    t = min(n, want)
    assert n % t == 0, (n, want)
    return t


def _rmsnorm_rows(x, g):
    ms = jnp.mean(x * x, axis=-1, keepdims=True)
    return x * lax.rsqrt(ms + EPS) * g


def _rmsnorm_kernel(x_ref, g_ref, o_ref):
    o_ref[...] = _rmsnorm_rows(x_ref[...], g_ref[...]).astype(o_ref.dtype)


def rmsnorm_bf16(x, g):
    m, d = x.shape
    tm = _tile(m, 512)
    return pl.pallas_call(
        _rmsnorm_kernel,
        grid=(m // tm,),
        in_specs=[pl.BlockSpec((tm, d), lambda i: (i, 0)),
                  pl.BlockSpec((1, d), lambda i: (0, 0))],
        out_specs=pl.BlockSpec((tm, d), lambda i: (i, 0)),
        out_shape=jax.ShapeDtypeStruct((m, d), BF16),
        compiler_params=_params(1),
        name="rmsnorm",
    )(x, g.reshape(1, d))


def _shift_rows(u, prev, s):
    rolled = pltpu.roll(u, s, axis=0)
    rprev = pltpu.roll(prev, s, axis=0)
    rows = lax.broadcasted_iota(jnp.int32, prev.shape, 0)
    first = jnp.where(rows < s, rprev, rolled[0:SUBLANES])
    return jnp.concatenate([first, rolled[SUBLANES:]], axis=0)


def _causal_conv3(u, prev, w):
    return (w[2:3] * u + w[1:2] * _shift_rows(u, prev, 1) + w[0:1] * _shift_rows(u, prev, 2))


def _load_carry(carry_ref, j, at_seq_start):
    @pl.when(at_seq_start)
    def _():
        carry_ref[j] = jnp.zeros(carry_ref.shape[1:], carry_ref.dtype)
    return carry_ref[j]


def _conv_mixer_kernel(h_ref, wb_ref, wc_ref, wv_ref, cw_ref, o_ref, carry_ref, *, tiles_per_seq):
    i = pl.program_id(0)
    j = pl.program_id(1)
    h = h_ref[...]
    u = _dot(h, wc_ref[...]) * _dot(h, wv_ref[...])
    prev = _load_carry(carry_ref, j, i % tiles_per_seq == 0)
    conv = _causal_conv3(u, prev, cw_ref[...])
    carry_ref[j] = u[u.shape[0] - SUBLANES:, :]
    o_ref[...] = (_dot(h, wb_ref[...]) * conv).astype(o_ref.dtype)


def conv_mixer(h, w_in, conv_w, seq):
    m, d = h.shape
    tm = _tile(seq, 1024)
    tn = 512
    nj = CONV_WIDTH // tn
    kern = functools.partial(_conv_mixer_kernel, tiles_per_seq=seq // tm)
    return pl.pallas_call(
        kern,
        grid=(m // tm, nj),
        in_specs=[pl.BlockSpec((tm, d), lambda i, j: (i, 0)),
                  pl.BlockSpec((d, tn), lambda i, j: (0, j)),
                  pl.BlockSpec((d, tn), lambda i, j: (0, nj + j)),
                  pl.BlockSpec((d, tn), lambda i, j: (0, 2 * nj + j)),
                  pl.BlockSpec((CONV_K, tn), lambda i, j: (0, j))],
        out_specs=pl.BlockSpec((tm, tn), lambda i, j: (i, j)),
        out_shape=jax.ShapeDtypeStruct((m, CONV_WIDTH), BF16),
        scratch_shapes=[pltpu.VMEM((nj, SUBLANES, tn), F32)],
        compiler_params=_params(2),
        name="conv_mixer",
    )(h, w_in, w_in, w_in, conv_w)


def _proj_kernel(h_ref, w_ref, scale_ref, o_ref):
    o_ref[...] = (_dot(h_ref[...], w_ref[...]) * scale_ref[...]).astype(o_ref.dtype)


def project(h, w, col_scale, col_block0, tm_want=1024, tn=1024):
    m, d = h.shape
    n_out = col_scale.shape[0]
    tm = _tile(m, tm_want)
    return pl.pallas_call(
        _proj_kernel,
        grid=(m // tm, n_out // tn),
        in_specs=[pl.BlockSpec((tm, d), lambda i, j: (i, 0)),
                  pl.BlockSpec((d, tn), lambda i, j: (0, col_block0 + j)),
                  pl.BlockSpec((1, tn), lambda i, j: (0, j))],
        out_specs=pl.BlockSpec((tm, tn), lambda i, j: (i, j)),
        out_shape=jax.ShapeDtypeStruct((m, n_out), BF16),
        compiler_params=_params(2),
        name="project",
    )(h, w, col_scale.reshape(1, n_out))


def _col_max(s):
    r = s.shape[0]
    while r > SUBLANES:
        r //= 2
        s = jnp.maximum(s[:r], s[r:])
    return jnp.max(s, axis=0, keepdims=True)


def _diff_attn_kernel(lam_ref, q_ref, k_ref, v_ref, gsub_ref, o_ref,
                      vt_ref, s_ref, p_ref, alpha_ref, m_ref, excess_ref, acc_ref, *, tq, tk, lambda_init):
    qi = pl.program_id(2)
    seq = k_ref.shape[1]
    dv = DIFF_V_DIM
    assert tq == 2 * tk

    @pl.when(qi == 0)
    def _():
        vt_ref[dv:, :] = jnp.ones((BF16_ROWS, seq), BF16)

        def body(c, carry):
            r = pl.multiple_of(c * tq, tq)
            vt_ref[0:dv, pl.ds(r, tq)] = v_ref[0, pl.ds(r, tq), :].astype(F32).T.astype(BF16)
            return carry
        lax.fori_loop(0, seq // tq, body, 0)

    q = q_ref[0].astype(F32)
    lane = lax.broadcasted_iota(jnp.int32, q.shape, 1)
    qz = (jnp.where(lane < DIFF_QK_DIM, q, 0.0).astype(BF16),
          jnp.where(lane >= DIFF_QK_DIM, q, 0.0).astype(BF16))

    def k_tile(t):
        return k_ref[0, pl.ds(pl.multiple_of(t * tk, tk), tk), :]

    def vt_tile(t):
        return vt_ref[:, pl.ds(pl.multiple_of(t * tk, tk), tk)]

    key_chunk = lax.broadcasted_iota(jnp.int32, (tk, tq), 0) // CHUNK
    qry_chunk = lax.broadcasted_iota(jnp.int32, (tk, tq), 1) // CHUNK
    diag_masks = (key_chunk <= qry_chunk, key_chunk + tk // CHUNK <= qry_chunk)

    def exact_tile(t):
        k_t = k_tile(t)
        vt_t = vt_tile(t)
        for c in range(2):
            s = _dot_nt(k_t, qz[c])
            m_old = m_ref[c]
            m_new = jnp.maximum(m_old, _col_max(s))
            p = jnp.exp2(s - m_new).astype(BF16)
            acc_ref[c] = jnp.exp2(m_old - m_new) * acc_ref[c] + _dot(vt_t, p)
            m_ref[c] = m_new

    def exact_diagonal():
        k_d = [k_tile(2 * qi + d) for d in range(2)]
        vt_d = [vt_tile(2 * qi + d) for d in range(2)]
        for c in range(2):
            s = [jnp.where(diag_masks[d], _dot_nt(k_d[d], qz[c]), -jnp.inf) for d in range(2)]
            m = jnp.maximum(_col_max(s[0]), _col_max(s[1]))
            acc_ref[c] = (_dot(vt_d[0], jnp.exp2(s[0] - m).astype(BF16))
                          + _dot(vt_d[1], jnp.exp2(s[1] - m).astype(BF16)))
            m_ref[c] = m

    def fast_group(t0, n):
        assert 2 <= n <= GROUP
        slots = range(GROUP - n, GROUP)
        m_frozen = [m_ref[c] for c in range(2)]
        tile_max = [None, None]
        k_next = k_tile(t0 + n)
        s_next = [_dot_nt(k_next, qz[c]) for c in range(2)]
        for c in range(2):
            s = s_ref[c]
            tile_max[c] = _col_max(s)
            p_ref[slots[0], c] = jnp.exp2(s - m_frozen[c]).astype(BF16)
            s_ref[c] = s_next[c]
        vt_pending = vt_tile(jnp.maximum(t0 - 1, 0))
        acc = [alpha_ref[c] * (acc_ref[c] + _dot(vt_pending, p_ref[GROUP - 1, c])) for c in range(2)]
        for i in range(1, n):
            k_t = k_tile(t0 + i)
            for c in range(2):
                s = _dot_nt(k_t, qz[c])
                tile_max[c] = jnp.maximum(tile_max[c], _col_max(s))
                p_ref[slots[i], c] = jnp.exp2(s - m_frozen[c]).astype(BF16)
            vt_t = vt_tile(t0 + i - 1)
            for c in range(2):
                acc[c] = acc[c] + _dot(vt_t, p_ref[slots[i - 1], c])
        for c in range(2):
            acc_ref[c] = acc[c]
            m_new = jnp.maximum(m_frozen[c], tile_max[c])
            alpha_ref[c] = jnp.exp2(m_frozen[c] - m_new)
            excess_ref[c] = jnp.maximum(excess_ref[c], tile_max[c] - m_frozen[c])
            m_ref[c] = m_new

    exact_diagonal()
    for c in range(2):
        s_ref[c] = _dot_nt(k_tile(0), qz[c])
    p_ref[GROUP - 1] = jnp.zeros(p_ref.shape[1:], BF16)
    alpha_ref[...] = jnp.ones(alpha_ref.shape, F32)
    excess_ref[...] = jnp.zeros(excess_ref.shape, F32)
    n_tiles = 2 * qi
    odd = n_tiles % GROUP

    @pl.when(odd > 0)
    def _():
        fast_group(0, 2)

    def group(j, carry):
        fast_group(odd + GROUP * j, GROUP)
        return carry
    lax.fori_loop(0, n_tiles // GROUP, group, 0)
    vt_last = vt_tile(jnp.maximum(n_tiles - 1, 0))
    for c in range(2):
        acc_ref[c] = alpha_ref[c] * (acc_ref[c] + _dot(vt_last, p_ref[GROUP - 1, c]))

    worst = jnp.max(jnp.maximum(excess_ref[0], excess_ref[1]), axis=-1, keepdims=True)

    @pl.when(worst[0, 0] > MAX_EXCESS)
    def _():
        exact_diagonal()

        def body(t, carry):
            exact_tile(t)
            return carry
        lax.fori_loop(0, 2 * qi, body, 0)

    lv = lam_ref[...]
    lam = (jnp.exp(jnp.sum(lv[0:1] * lv[1:2], axis=-1, keepdims=True))
           - jnp.exp(jnp.sum(lv[2:3] * lv[3:4], axis=-1, keepdims=True)) + lambda_init)
    a1 = acc_ref[0]
    a2 = acc_ref[1]
    o = a1[0:dv] / a1[dv:dv + 1] - lam * (a2[0:dv] / a2[dv:dv + 1])
    ms = jnp.mean(o * o, axis=0, keepdims=True)
    y = o * lax.rsqrt(ms + EPS) * gsub_ref[...]
    y = y * (1.0 - lambda_init)
    o_ref[0] = y.T.astype(o_ref.dtype)


def diff_attention(qkvx, lam_rows, g_sub, bsz, seq, lambda_init):
    tq = _tile(seq, 512)
    tk = tq // 2
    nq = seq // tq
    dv = DIFF_V_DIM
    kern = functools.partial(_diff_attn_kernel, tq=tq, tk=tk, lambda_init=lambda_init)
    return pl.pallas_call(
        kern,
        grid=(bsz, DIFF_HEADS, nq),
        in_specs=[pl.BlockSpec((4, DIFF_QK_DIM), lambda b, h, i: (0, 0)),
                  pl.BlockSpec((1, tq, dv), lambda b, h, i: (b, i, h)),
                  pl.BlockSpec((1, seq, dv), lambda b, h, i: (b, 0, DIFF_HEADS + h)),
                  pl.BlockSpec((1, seq, dv), lambda b, h, i: (b, 0, 2 * DIFF_HEADS + h)),
                  pl.BlockSpec((dv, 1), lambda b, h, i: (0, 0))],
        out_specs=pl.BlockSpec((1, tq, dv), lambda b, h, i: (b, i, h)),
        out_shape=jax.ShapeDtypeStruct((bsz, seq, DIFF_WIDTH), BF16),
        scratch_shapes=[pltpu.VMEM((dv + BF16_ROWS, seq), BF16),
                        pltpu.VMEM((2, tk, tq), F32),
                        pltpu.VMEM((GROUP, 2, tk, tq), BF16),
                        pltpu.VMEM((2, 1, tq), F32),
                        pltpu.VMEM((2, 1, tq), F32),
                        pltpu.VMEM((2, 1, tq), F32),
                        pltpu.VMEM((2, dv + BF16_ROWS, tq), F32)],
        compiler_params=_params(3),
        name="diff_attention",
    )(lam_rows, qkvx, qkvx, qkvx, g_sub.reshape(dv, 1))


def _cross_attn_kernel(xq_ref, kv_ref, o_ref):
    hd = CROSS_HEAD_DIM
    for h in range(CROSS_HEADS):
        q = xq_ref[0, :, h * hd:(h + 1) * hd]
        k = kv_ref[0, :, h * hd:(h + 1) * hd]
        v = kv_ref[0, :, CROSS_WIDTH + h * hd:CROSS_WIDTH + (h + 1) * hd]
        s = _dot_nt(q, k) * (hd ** -0.5)
        p = jnp.exp(s - jnp.max(s, axis=-1, keepdims=True))
        l = jnp.sum(p, axis=-1, keepdims=True)
        o_ref[0, :, h * hd:(h + 1) * hd] = (_dot(p.astype(BF16), v) / l).astype(o_ref.dtype)


def cross_attention(qkvx, kv, bsz, seq):
    tm = _tile(seq, 512)
    mem_len = kv.shape[1]
    return pl.pallas_call(
        _cross_attn_kernel,
        grid=(bsz, seq // tm),
        in_specs=[pl.BlockSpec((1, tm, CROSS_WIDTH), lambda b, i: (b, i, 3)),
                  pl.BlockSpec((1, mem_len, 2 * CROSS_WIDTH), lambda b, i: (b, 0, 0))],
        out_specs=pl.BlockSpec((1, tm, CROSS_WIDTH), lambda b, i: (b, i, 0)),
        out_shape=jax.ShapeDtypeStruct((bsz, seq, CROSS_WIDTH), BF16),
        compiler_params=_params(2),
        name="cross_attention",
    )(qkvx, kv)


def _merge_kernel(h_ref, ya_ref, yd_ref, yx_ref, wga_ref, wgb_ref, wgc_ref,
                  ba_ref, bb_ref, bc_ref, wa_ref, wd_ref, wx_ref, o_ref):
    h = h_ref[...]
    merged = jax.nn.sigmoid(_dot(h, wga_ref[...]) + ba_ref[...]) * _dot(ya_ref[...], wa_ref[...])
    merged += jax.nn.sigmoid(_dot(h, wgb_ref[...]) + bb_ref[...]) * _dot(yd_ref[...], wd_ref[...])
    merged += jax.nn.sigmoid(_dot(h, wgc_ref[...]) + bc_ref[...]) * _dot(yx_ref[...], wx_ref[...])
    o_ref[...] = merged.astype(o_ref.dtype)


def gated_merge(h, ya, yd, yx, w_gate, b_gate, w_a, w_d, w_x):
    m, d = h.shape
    tm = _tile(m, 1024)
    tn = 512
    nj = d // tn
    act = lambda width: pl.BlockSpec((tm, width), lambda i, j: (i, 0))
    gate_w = lambda br: pl.BlockSpec((d, tn), lambda i, j: (0, br * nj + j))
    gate_b = lambda br: pl.BlockSpec((1, tn), lambda i, j: (0, br * nj + j))
    out_w = lambda width: pl.BlockSpec((width, tn), lambda i, j: (0, j))
    return pl.pallas_call(
        _merge_kernel,
        grid=(m // tm, nj),
        in_specs=[act(d), act(CONV_WIDTH), act(DIFF_WIDTH), act(CROSS_WIDTH),
                  gate_w(0), gate_w(1), gate_w(2), gate_b(0), gate_b(1), gate_b(2),
                  out_w(CONV_WIDTH), out_w(DIFF_WIDTH), out_w(CROSS_WIDTH)],
        out_specs=pl.BlockSpec((tm, tn), lambda i, j: (i, j)),
        out_shape=jax.ShapeDtypeStruct((m, d), BF16),
        compiler_params=_params(2),
        name="gated_merge",
    )(h, ya, yd, yx, w_gate, w_gate, w_gate, b_gate, b_gate, b_gate, w_a, w_d, w_x)


def _out_proj_kernel(z_ref, w_ref, x_ref, gpost_ref, gpre_ref, x1_ref, h2_ref):
    for r0 in range(0, z_ref.shape[0], EPILOGUE_ROWS):
        rows = slice(r0, r0 + EPILOGUE_ROWS)
        y = _dot(z_ref[rows, :], w_ref[...])
        x1 = x_ref[rows, :] + _rmsnorm_rows(y, gpost_ref[...])
        x1_ref[rows, :] = x1
        h2_ref[rows, :] = _rmsnorm_rows(x1, gpre_ref[...]).astype(h2_ref.dtype)


def out_proj_residual(z, w_o, x, g_post, g_next_pre):
    m, d = x.shape
    tm = _tile(m, 512)
    row = lambda: pl.BlockSpec((tm, d), lambda i: (i, 0))
    vec = lambda: pl.BlockSpec((1, d), lambda i: (0, 0))
    return pl.pallas_call(
        _out_proj_kernel,
        grid=(m // tm,),
        in_specs=[row(), pl.BlockSpec((d, d), lambda i: (0, 0)), row(), vec(), vec()],
        out_specs=[row(), row()],
        out_shape=[jax.ShapeDtypeStruct((m, d), F32), jax.ShapeDtypeStruct((m, d), BF16)],
        compiler_params=_params(1),
        name="out_proj_residual",
    )(z, w_o, x, g_post.reshape(1, d), g_next_pre.reshape(1, d))


def _ffn_up_kernel(h_ref, wg_ref, wv_ref, cwg_ref, cwv_ref, o_ref, carry_g, carry_v, *, tiles_per_seq):
    i = pl.program_id(0)
    j = pl.program_id(1)
    start = i % tiles_per_seq == 0
    prev_g = _load_carry(carry_g, j, start)
    prev_v = _load_carry(carry_v, j, start)
    for r0 in range(0, h_ref.shape[0], EPILOGUE_ROWS):
        h = h_ref[r0:r0 + EPILOGUE_ROWS, :]
        zg = _dot(h, wg_ref[...])
        zv = _dot(h, wv_ref[...])
        gate = _causal_conv3(zg, prev_g, cwg_ref[...])
        val = _causal_conv3(zv, prev_v, cwv_ref[...])
        o_ref[r0:r0 + EPILOGUE_ROWS, :] = (gate * jax.nn.sigmoid(gate) * val).astype(o_ref.dtype)
        prev_g = zg[EPILOGUE_ROWS - SUBLANES:, :]
        prev_v = zv[EPILOGUE_ROWS - SUBLANES:, :]
    carry_g[j] = prev_g
    carry_v[j] = prev_v


def ffn_up(h, w_up, conv_w, seq):
    m, d = h.shape
    d_ff = w_up.shape[1] // 2
    tm = _tile(seq, 1024)
    tn = 512
    nj = d_ff // tn
    kern = functools.partial(_ffn_up_kernel, tiles_per_seq=seq // tm)
    return pl.pallas_call(
        kern,
        grid=(m // tm, nj),
        in_specs=[pl.BlockSpec((tm, d), lambda i, j: (i, 0)),
                  pl.BlockSpec((d, tn), lambda i, j: (0, j)),
                  pl.BlockSpec((d, tn), lambda i, j: (0, nj + j)),
                  pl.BlockSpec((CONV_K, tn), lambda i, j: (0, j)),
                  pl.BlockSpec((CONV_K, tn), lambda i, j: (0, nj + j))],
        out_specs=pl.BlockSpec((tm, tn), lambda i, j: (i, j)),
        out_shape=jax.ShapeDtypeStruct((m, d_ff), BF16),
        scratch_shapes=[pltpu.VMEM((nj, SUBLANES, tn), F32), pltpu.VMEM((nj, SUBLANES, tn), F32)],
        compiler_params=_params(2),
        name="ffn_up",
    )(h, w_up, w_up, conv_w, conv_w)


def _ffn_down_kernel(a_ref, w_ref, x_ref, g_ref, o_ref, acc_ref):
    k = pl.program_id(1)
    last = pl.num_programs(1) - 1

    @pl.when(k == 0)
    def _():
        acc_ref[...] = _dot(a_ref[...], w_ref[...])

    @pl.when((k > 0) & (k < last))
    def _():
        acc_ref[...] += _dot(a_ref[...], w_ref[...])

    @pl.when(k == last)
    def _():
        for r0 in range(0, a_ref.shape[0], EPILOGUE_ROWS):
            rows = slice(r0, r0 + EPILOGUE_ROWS)
            y = acc_ref[rows, :] + _dot(a_ref[rows, :], w_ref[...])
            o_ref[rows, :] = x_ref[rows, :] + _rmsnorm_rows(y, g_ref[...])


def ffn_down(a, w_down, x, g_post):
    m, d = x.shape
    d_ff = a.shape[1]
    tm = _tile(m, 1024)
    tk = 512
    assert d_ff // tk >= 2
    row = lambda: pl.BlockSpec((tm, d), lambda i, k: (i, 0))
    return pl.pallas_call(
        _ffn_down_kernel,
        grid=(m // tm, d_ff // tk),
        in_specs=[pl.BlockSpec((tm, tk), lambda i, k: (i, k)),
                  pl.BlockSpec((tk, d), lambda i, k: (k, 0)),
                  row(), pl.BlockSpec((1, d), lambda i, k: (0, 0))],
        out_specs=row(),
        out_shape=jax.ShapeDtypeStruct((m, d), F32),
        scratch_shapes=[pltpu.VMEM((tm, d), F32)],
        compiler_params=_params(2),
        name="ffn_down",
    )(a, w_down, x, g_post.reshape(1, d))


def kernel(x, mem, g_mix_pre, w_in, w_gate, b_gate, conv_w, w_conv_out, lambda_q1, lambda_k1, lambda_q2, lambda_k2, g_diff_sub, w_diff_out, g_mem, w_mem_kv, w_cross_out, w_o, g_mix_post, g_ffn_pre, w_up, ffn_conv_w, w_down, g_ffn_post):
    bsz, seq, d = x.shape
    mem_len = mem.shape[1]
    depth = w_in.shape[0]
    m = bsz * seq
    q_lo = N_BRANCH * CONV_WIDTH
    qkvx_width = 2 * DIFF_QK_WIDTH + DIFF_WIDTH + CROSS_WIDTH
    qkvx_scale = jnp.where(jnp.arange(qkvx_width) < DIFF_QK_WIDTH,
                           LOG2E * DIFF_QK_DIM ** -0.5, 1.0).astype(F32)
    kv_scale = jnp.ones((2 * CROSS_WIDTH,), F32)

    xf = x.reshape(m, d)
    h = rmsnorm_bf16(xf, g_mix_pre[0])
    for l in range(depth):
        lambda_init = 0.8 - 0.6 * math.exp(-0.3 * l)
        w_in_b = w_in[l].astype(BF16)
        ya = conv_mixer(h, w_in_b, conv_w[l], seq)
        qkvx = project(h, w_in_b, qkvx_scale, q_lo // 1024).reshape(bsz, seq, qkvx_width)

        lam_rows = jnp.stack([lambda_q1[l], lambda_k1[l], lambda_q2[l], lambda_k2[l]])
        yd = diff_attention(qkvx, lam_rows, g_diff_sub[l], bsz, seq, lambda_init)

        mem_n = rmsnorm_bf16(mem.reshape(bsz * mem_len, d), g_mem[l])
        kv = project(mem_n, w_mem_kv[l].astype(BF16), kv_scale, 0, tm_want=bsz * mem_len)
        yx = cross_attention(qkvx, kv.reshape(bsz, mem_len, 2 * CROSS_WIDTH), bsz, seq)

        merged = gated_merge(h, ya, yd.reshape(m, DIFF_WIDTH), yx.reshape(m, CROSS_WIDTH),
                             w_gate[l].astype(BF16), b_gate[l].reshape(1, -1),
                             w_conv_out[l].astype(BF16), w_diff_out[l].astype(BF16),
                             w_cross_out[l].astype(BF16))
        xf, h2 = out_proj_residual(merged, w_o[l].astype(BF16), xf, g_mix_post[l], g_ffn_pre[l])

        act = ffn_up(h2, w_up[l].astype(BF16), ffn_conv_w[l], seq)
        xf = ffn_down(act, w_down[l].astype(BF16), xf, g_ffn_post[l])
        if l + 1 < depth:
            h = rmsnorm_bf16(xf, g_mix_pre[l + 1])
    return xf.reshape(bsz, seq, d)
```

```python
import functools
import math

import jax
import jax.numpy as jnp
from jax import lax
from jax.experimental import pallas as pl
from jax.experimental.pallas import tpu as pltpu

EPS = 1e-6
CHUNK = 64
CONV_WIDTH = 1024
CONV_K = 3
DIFF_HEADS = 8
DIFF_QK_DIM = 64
DIFF_V_DIM = 2 * DIFF_QK_DIM
DIFF_QK_WIDTH = DIFF_HEADS * 2 * DIFF_QK_DIM
DIFF_WIDTH = DIFF_HEADS * DIFF_V_DIM
CROSS_HEADS = 4
CROSS_HEAD_DIM = 256
CROSS_WIDTH = CROSS_HEADS * CROSS_HEAD_DIM
N_BRANCH = 3
LOG2E = math.log2(math.e)
MAX_EXCESS = 30.0
GROUP = 8

SUBLANES = 8
BF16_ROWS = 16
EPILOGUE_ROWS = 256
VMEM_LIMIT = 56 * 1024 * 1024

F32 = jnp.float32
BF16 = jnp.bfloat16


def _params(n_axes):
    return pltpu.CompilerParams(
        dimension_semantics=("arbitrary",) * n_axes, vmem_limit_bytes=VMEM_LIMIT)


def _dot(a, b):
    return jnp.dot(a, b, preferred_element_type=F32)


def _dot_nt(a, b):
    return lax.dot_general(a, b, (((1,), (1,)), ((), ())), preferred_element_type=F32)


def _tile(n, want):
    t = min(n, want)
    assert n % t == 0, (n, want)
    return t


def _rmsnorm_rows(x, g):
    ms = jnp.mean(x * x, axis=-1, keepdims=True)
    return x * lax.rsqrt(ms + EPS) * g


def _rmsnorm_kernel(x_ref, g_ref, o_ref):
    o_ref[...] = _rmsnorm_rows(x_ref[...], g_ref[...]).astype(o_ref.dtype)


def rmsnorm_bf16(x, g):
    m, d = x.shape
    tm = _tile(m, 512)
    return pl.pallas_call(
        _rmsnorm_kernel,
        grid=(m // tm,),
        in_specs=[pl.BlockSpec((tm, d), lambda i: (i, 0)),
                  pl.BlockSpec((1, d), lambda i: (0, 0))],
        out_specs=pl.BlockSpec((tm, d), lambda i: (i, 0)),
        out_shape=jax.ShapeDtypeStruct((m, d), BF16),
        compiler_params=_params(1),
        name="rmsnorm",
    )(x, g.reshape(1, d))


def _shift_rows(u, prev, s):
    rolled = pltpu.roll(u, s, axis=0)
    rprev = pltpu.roll(prev, s, axis=0)
    rows = lax.broadcasted_iota(jnp.int32, prev.shape, 0)
    first = jnp.where(rows < s, rprev, rolled[0:SUBLANES])
    return jnp.concatenate([first, rolled[SUBLANES:]], axis=0)


def _causal_conv3(u, prev, w):
    return (w[2:3] * u + w[1:2] * _shift_rows(u, prev, 1) + w[0:1] * _shift_rows(u, prev, 2))


def _load_carry(carry_ref, j, at_seq_start):
    @pl.when(at_seq_start)
    def _():
        carry_ref[j] = jnp.zeros(carry_ref.shape[1:], carry_ref.dtype)
    return carry_ref[j]


def _conv_mixer_kernel(h_ref, wb_ref, wc_ref, wv_ref, cw_ref, o_ref, carry_ref, *, tiles_per_seq):
    i = pl.program_id(0)
    j = pl.program_id(1)
    h = h_ref[...]
    u = _dot(h, wc_ref[...]) * _dot(h, wv_ref[...])
    prev = _load_carry(carry_ref, j, i % tiles_per_seq == 0)
    conv = _causal_conv3(u, prev, cw_ref[...])
    carry_ref[j] = u[u.shape[0] - SUBLANES:, :]
    o_ref[...] = (_dot(h, wb_ref[...]) * conv).astype(o_ref.dtype)


def conv_mixer(h, w_in, conv_w, seq):
    m, d = h.shape
    tm = _tile(seq, 1024)
    tn = 512
    nj = CONV_WIDTH // tn
    kern = functools.partial(_conv_mixer_kernel, tiles_per_seq=seq // tm)
    return pl.pallas_call(
        kern,
        grid=(m // tm, nj),
        in_specs=[pl.BlockSpec((tm, d), lambda i, j: (i, 0)),
                  pl.BlockSpec((d, tn), lambda i, j: (0, j)),
                  pl.BlockSpec((d, tn), lambda i, j: (0, nj + j)),
                  pl.BlockSpec((d, tn), lambda i, j: (0, 2 * nj + j)),
                  pl.BlockSpec((CONV_K, tn), lambda i, j: (0, j))],
        out_specs=pl.BlockSpec((tm, tn), lambda i, j: (i, j)),
        out_shape=jax.ShapeDtypeStruct((m, CONV_WIDTH), BF16),
        scratch_shapes=[pltpu.VMEM((nj, SUBLANES, tn), F32)],
        compiler_params=_params(2),
        name="conv_mixer",
    )(h, w_in, w_in, w_in, conv_w)


def _proj_kernel(h_ref, w_ref, scale_ref, o_ref):
    o_ref[...] = (_dot(h_ref[...], w_ref[...]) * scale_ref[...]).astype(o_ref.dtype)


def project(h, w, col_scale, col_block0, tm_want=1024, tn=1024):
    m, d = h.shape
    n_out = col_scale.shape[0]
    tm = _tile(m, tm_want)
    return pl.pallas_call(
        _proj_kernel,
        grid=(m // tm, n_out // tn),
        in_specs=[pl.BlockSpec((tm, d), lambda i, j: (i, 0)),
                  pl.BlockSpec((d, tn), lambda i, j: (0, col_block0 + j)),
                  pl.BlockSpec((1, tn), lambda i, j: (0, j))],
        out_specs=pl.BlockSpec((tm, tn), lambda i, j: (i, j)),
        out_shape=jax.ShapeDtypeStruct((m, n_out), BF16),
        compiler_params=_params(2),
        name="project",
    )(h, w, col_scale.reshape(1, n_out))


def _col_max(s):
    r = s.shape[0]
    while r > SUBLANES:
        r //= 2
        s = jnp.maximum(s[:r], s[r:])
    return jnp.max(s, axis=0, keepdims=True)


def _diff_attn_kernel(lam_ref, q_ref, k_ref, v_ref, gsub_ref, o_ref,
                      vt_ref, s_ref, p_ref, alpha_ref, m_ref, excess_ref, acc_ref, *, tq, tk, lambda_init):
    qi = pl.program_id(2)
    seq = k_ref.shape[1]
    dv = DIFF_V_DIM
    assert tq == 2 * tk

    @pl.when(qi == 0)
    def _():
        vt_ref[dv:, :] = jnp.ones((BF16_ROWS, seq), BF16)

        def body(c, carry):
            r = pl.multiple_of(c * tq, tq)
            vt_ref[0:dv, pl.ds(r, tq)] = v_ref[0, pl.ds(r, tq), :].astype(F32).T.astype(BF16)
            return carry
        lax.fori_loop(0, seq // tq, body, 0)

    q = q_ref[0].astype(F32)
    lane = lax.broadcasted_iota(jnp.int32, q.shape, 1)
    qz = (jnp.where(lane < DIFF_QK_DIM, q, 0.0).astype(BF16),
          jnp.where(lane >= DIFF_QK_DIM, q, 0.0).astype(BF16))

    def k_tile(t):
        return k_ref[0, pl.ds(pl.multiple_of(t * tk, tk), tk), :]

    def vt_tile(t):
        return vt_ref[:, pl.ds(pl.multiple_of(t * tk, tk), tk)]

    key_chunk = lax.broadcasted_iota(jnp.int32, (tk, tq), 0) // CHUNK
    qry_chunk = lax.broadcasted_iota(jnp.int32, (tk, tq), 1) // CHUNK
    diag_masks = (key_chunk <= qry_chunk, key_chunk + tk // CHUNK <= qry_chunk)

    def exact_tile(t):
        k_t = k_tile(t)
        vt_t = vt_tile(t)
        for c in range(2):
            s = _dot_nt(k_t, qz[c])
            m_old = m_ref[c]
            m_new = jnp.maximum(m_old, _col_max(s))
            p = jnp.exp2(s - m_new).astype(BF16)
            acc_ref[c] = jnp.exp2(m_old - m_new) * acc_ref[c] + _dot(vt_t, p)
            m_ref[c] = m_new

    def exact_diagonal():
        k_d = [k_tile(2 * qi + d) for d in range(2)]
        vt_d = [vt_tile(2 * qi + d) for d in range(2)]
        for c in range(2):
            s = [jnp.where(diag_masks[d], _dot_nt(k_d[d], qz[c]), -jnp.inf) for d in range(2)]
            m = jnp.maximum(_col_max(s[0]), _col_max(s[1]))
            acc_ref[c] = (_dot(vt_d[0], jnp.exp2(s[0] - m).astype(BF16))
                          + _dot(vt_d[1], jnp.exp2(s[1] - m).astype(BF16)))
            m_ref[c] = m

    def fast_group(t0, n, masks=None):
        assert 2 <= n <= GROUP
        slots = range(GROUP - n, GROUP)
        m_frozen = [m_ref[c] for c in range(2)]
        tile_max = [None, None]
        if masks is None:
            k_next = k_tile(t0 + n)
            s_next = [_dot_nt(k_next, qz[c]) for c in range(2)]
        for c in range(2):
            s = s_ref[c]
            if masks is not None:
                s = jnp.where(masks[0], s, -jnp.inf)
            tile_max[c] = _col_max(s)
            p_ref[slots[0], c] = jnp.exp2(s - m_frozen[c]).astype(BF16)
            if masks is None:
                s_ref[c] = s_next[c]
        vt_pending = vt_tile(jnp.maximum(t0 - 1, 0))
        acc = [alpha_ref[c] * (acc_ref[c] + _dot(vt_pending, p_ref[GROUP - 1, c])) for c in range(2)]
        for i in range(1, n):
            k_t = k_tile(t0 + i)
            for c in range(2):
                s = _dot_nt(k_t, qz[c])
                if masks is not None:
                    s = jnp.where(masks[i], s, -jnp.inf)
                tile_max[c] = jnp.maximum(tile_max[c], _col_max(s))
                p_ref[slots[i], c] = jnp.exp2(s - m_frozen[c]).astype(BF16)
            vt_t = vt_tile(t0 + i - 1)
            for c in range(2):
                acc[c] = acc[c] + _dot(vt_t, p_ref[slots[i - 1], c])
        for c in range(2):
            acc_ref[c] = acc[c]
            m_new = jnp.maximum(m_frozen[c], tile_max[c])
            alpha_ref[c] = jnp.exp2(m_frozen[c] - m_new)
            excess_ref[c] = jnp.maximum(excess_ref[c], tile_max[c] - m_frozen[c])
            m_ref[c] = m_new

    for c in range(2):
        s0 = _dot_nt(k_tile(0), qz[c])
        s_ref[c] = s0
        m_ref[c] = _col_max(s0[0:CHUNK])
    acc_ref[...] = jnp.zeros(acc_ref.shape, F32)
    p_ref[GROUP - 1] = jnp.zeros(p_ref.shape[1:], BF16)
    alpha_ref[...] = jnp.ones(alpha_ref.shape, F32)
    excess_ref[...] = jnp.zeros(excess_ref.shape, F32)
    n_tiles = 2 * qi
    rem = n_tiles % GROUP

    @pl.when(rem % 4 > 0)
    def _():
        fast_group(0, 2)

    @pl.when(rem >= 4)
    def _():
        fast_group(rem % 4, 4)

    def group(j, carry):
        fast_group(rem + GROUP * j, GROUP)
        return carry
    lax.fori_loop(0, n_tiles // GROUP, group, 0)
    fast_group(n_tiles, 2, diag_masks)
    vt_last = vt_tile(n_tiles + 1)
    for c in range(2):
        acc_ref[c] = alpha_ref[c] * (acc_ref[c] + _dot(vt_last, p_ref[GROUP - 1, c]))

    worst = jnp.max(jnp.maximum(excess_ref[0], excess_ref[1]), axis=-1, keepdims=True)

    @pl.when(worst[0, 0] > MAX_EXCESS)
    def _():
        exact_diagonal()

        def body(t, carry):
            exact_tile(t)
            return carry
        lax.fori_loop(0, 2 * qi, body, 0)

    lv = lam_ref[...]
    lam = (jnp.exp(jnp.sum(lv[0:1] * lv[1:2], axis=-1, keepdims=True))
           - jnp.exp(jnp.sum(lv[2:3] * lv[3:4], axis=-1, keepdims=True)) + lambda_init)
    a1 = acc_ref[0]
    a2 = acc_ref[1]
    o = a1[0:dv] / a1[dv:dv + 1] - lam * (a2[0:dv] / a2[dv:dv + 1])
    ms = jnp.mean(o * o, axis=0, keepdims=True)
    y = o * lax.rsqrt(ms + EPS) * gsub_ref[...]
    y = y * (1.0 - lambda_init)
    o_ref[0] = y.T.astype(o_ref.dtype)


def diff_attention(qkvx, lam_rows, g_sub, bsz, seq, lambda_init):
    tq = _tile(seq, 512)
    tk = tq // 2
    nq = seq // tq
    dv = DIFF_V_DIM
    kern = functools.partial(_diff_attn_kernel, tq=tq, tk=tk, lambda_init=lambda_init)
    return pl.pallas_call(
        kern,
        grid=(bsz, DIFF_HEADS, nq),
        in_specs=[pl.BlockSpec((4, DIFF_QK_DIM), lambda b, h, i: (0, 0)),
                  pl.BlockSpec((1, tq, dv), lambda b, h, i: (b, i, h)),
                  pl.BlockSpec((1, seq, dv), lambda b, h, i: (b, 0, DIFF_HEADS + h)),
                  pl.BlockSpec((1, seq, dv), lambda b, h, i: (b, 0, 2 * DIFF_HEADS + h)),
                  pl.BlockSpec((dv, 1), lambda b, h, i: (0, 0))],
        out_specs=pl.BlockSpec((1, tq, dv), lambda b, h, i: (b, i, h)),
        out_shape=jax.ShapeDtypeStruct((bsz, seq, DIFF_WIDTH), BF16),
        scratch_shapes=[pltpu.VMEM((dv + BF16_ROWS, seq), BF16),
                        pltpu.VMEM((2, tk, tq), F32),
                        pltpu.VMEM((GROUP, 2, tk, tq), BF16),
                        pltpu.VMEM((2, 1, tq), F32),
                        pltpu.VMEM((2, 1, tq), F32),
                        pltpu.VMEM((2, 1, tq), F32),
                        pltpu.VMEM((2, dv + BF16_ROWS, tq), F32)],
        compiler_params=_params(3),
        name="diff_attention",
    )(lam_rows, qkvx, qkvx, qkvx, g_sub.reshape(dv, 1))


def _cross_attn_kernel(xq_ref, kv_ref, o_ref):
    hd = CROSS_HEAD_DIM
    for h in range(CROSS_HEADS):
        q = xq_ref[0, :, h * hd:(h + 1) * hd]
        k = kv_ref[0, :, h * hd:(h + 1) * hd]
        v = kv_ref[0, :, CROSS_WIDTH + h * hd:CROSS_WIDTH + (h + 1) * hd]
        s = _dot_nt(q, k) * (hd ** -0.5)
        p = jnp.exp(s - jnp.max(s, axis=-1, keepdims=True))
        l = jnp.sum(p, axis=-1, keepdims=True)
        o_ref[0, :, h * hd:(h + 1) * hd] = (_dot(p.astype(BF16), v) / l).astype(o_ref.dtype)


def cross_attention(qkvx, kv, bsz, seq):
    tm = _tile(seq, 512)
    mem_len = kv.shape[1]
    return pl.pallas_call(
        _cross_attn_kernel,
        grid=(bsz, seq // tm),
        in_specs=[pl.BlockSpec((1, tm, CROSS_WIDTH), lambda b, i: (b, i, 3)),
                  pl.BlockSpec((1, mem_len, 2 * CROSS_WIDTH), lambda b, i: (b, 0, 0))],
        out_specs=pl.BlockSpec((1, tm, CROSS_WIDTH), lambda b, i: (b, i, 0)),
        out_shape=jax.ShapeDtypeStruct((bsz, seq, CROSS_WIDTH), BF16),
        compiler_params=_params(2),
        name="cross_attention",
    )(qkvx, kv)


def _merge_kernel(h_ref, ya_ref, yd_ref, yx_ref, wga_ref, wgb_ref, wgc_ref,
                  ba_ref, bb_ref, bc_ref, wa_ref, wd_ref, wx_ref, o_ref):
    h = h_ref[...]
    merged = jax.nn.sigmoid(_dot(h, wga_ref[...]) + ba_ref[...]) * _dot(ya_ref[...], wa_ref[...])
    merged += jax.nn.sigmoid(_dot(h, wgb_ref[...]) + bb_ref[...]) * _dot(yd_ref[...], wd_ref[...])
    merged += jax.nn.sigmoid(_dot(h, wgc_ref[...]) + bc_ref[...]) * _dot(yx_ref[...], wx_ref[...])
    o_ref[...] = merged.astype(o_ref.dtype)


def gated_merge(h, ya, yd, yx, w_gate, b_gate, w_a, w_d, w_x):
    m, d = h.shape
    tm = _tile(m, 1024)
    tn = 512
    nj = d // tn
    act = lambda width: pl.BlockSpec((tm, width), lambda i, j: (i, 0))
    gate_w = lambda br: pl.BlockSpec((d, tn), lambda i, j: (0, br * nj + j))
    gate_b = lambda br: pl.BlockSpec((1, tn), lambda i, j: (0, br * nj + j))
    out_w = lambda width: pl.BlockSpec((width, tn), lambda i, j: (0, j))
    return pl.pallas_call(
        _merge_kernel,
        grid=(m // tm, nj),
        in_specs=[act(d), act(CONV_WIDTH), act(DIFF_WIDTH), act(CROSS_WIDTH),
                  gate_w(0), gate_w(1), gate_w(2), gate_b(0), gate_b(1), gate_b(2),
                  out_w(CONV_WIDTH), out_w(DIFF_WIDTH), out_w(CROSS_WIDTH)],
        out_specs=pl.BlockSpec((tm, tn), lambda i, j: (i, j)),
        out_shape=jax.ShapeDtypeStruct((m, d), BF16),
        compiler_params=_params(2),
        name="gated_merge",
    )(h, ya, yd, yx, w_gate, w_gate, w_gate, b_gate, b_gate, b_gate, w_a, w_d, w_x)


def _out_proj_kernel(z_ref, w_ref, x_ref, gpost_ref, gpre_ref, x1_ref, h2_ref):
    for r0 in range(0, z_ref.shape[0], EPILOGUE_ROWS):
        rows = slice(r0, r0 + EPILOGUE_ROWS)
        y = _dot(z_ref[rows, :], w_ref[...])
        x1 = x_ref[rows, :] + _rmsnorm_rows(y, gpost_ref[...])
        x1_ref[rows, :] = x1
        h2_ref[rows, :] = _rmsnorm_rows(x1, gpre_ref[...]).astype(h2_ref.dtype)


def out_proj_residual(z, w_o, x, g_post, g_next_pre):
    m, d = x.shape
    tm = _tile(m, 512)
    row = lambda: pl.BlockSpec((tm, d), lambda i: (i, 0))
    vec = lambda: pl.BlockSpec((1, d), lambda i: (0, 0))
    return pl.pallas_call(
        _out_proj_kernel,
        grid=(m // tm,),
        in_specs=[row(), pl.BlockSpec((d, d), lambda i: (0, 0)), row(), vec(), vec()],
        out_specs=[row(), row()],
        out_shape=[jax.ShapeDtypeStruct((m, d), F32), jax.ShapeDtypeStruct((m, d), BF16)],
        compiler_params=_params(1),
        name="out_proj_residual",
    )(z, w_o, x, g_post.reshape(1, d), g_next_pre.reshape(1, d))


def _ffn_up_kernel(h_ref, wg_ref, wv_ref, cwg_ref, cwv_ref, o_ref, carry_g, carry_v, *, tiles_per_seq):
    i = pl.program_id(0)
    j = pl.program_id(1)
    start = i % tiles_per_seq == 0
    prev_g = _load_carry(carry_g, j, start)
    prev_v = _load_carry(carry_v, j, start)
    for r0 in range(0, h_ref.shape[0], EPILOGUE_ROWS):
        h = h_ref[r0:r0 + EPILOGUE_ROWS, :]
        zg = _dot(h, wg_ref[...])
        zv = _dot(h, wv_ref[...])
        gate = _causal_conv3(zg, prev_g, cwg_ref[...])
        val = _causal_conv3(zv, prev_v, cwv_ref[...])
        o_ref[r0:r0 + EPILOGUE_ROWS, :] = (gate * jax.nn.sigmoid(gate) * val).astype(o_ref.dtype)
        prev_g = zg[EPILOGUE_ROWS - SUBLANES:, :]
        prev_v = zv[EPILOGUE_ROWS - SUBLANES:, :]
    carry_g[j] = prev_g
    carry_v[j] = prev_v


def ffn_up(h, w_up, conv_w, seq):
    m, d = h.shape
    d_ff = w_up.shape[1] // 2
    tm = _tile(seq, 1024)
    tn = 512
    nj = d_ff // tn
    kern = functools.partial(_ffn_up_kernel, tiles_per_seq=seq // tm)
    return pl.pallas_call(
        kern,
        grid=(m // tm, nj),
        in_specs=[pl.BlockSpec((tm, d), lambda i, j: (i, 0)),
                  pl.BlockSpec((d, tn), lambda i, j: (0, j)),
                  pl.BlockSpec((d, tn), lambda i, j: (0, nj + j)),
                  pl.BlockSpec((CONV_K, tn), lambda i, j: (0, j)),
                  pl.BlockSpec((CONV_K, tn), lambda i, j: (0, nj + j))],
        out_specs=pl.BlockSpec((tm, tn), lambda i, j: (i, j)),
        out_shape=jax.ShapeDtypeStruct((m, d_ff), BF16),
        scratch_shapes=[pltpu.VMEM((nj, SUBLANES, tn), F32), pltpu.VMEM((nj, SUBLANES, tn), F32)],
        compiler_params=_params(2),
        name="ffn_up",
    )(h, w_up, w_up, conv_w, conv_w)


def _ffn_down_kernel(a_ref, w_ref, x_ref, g_ref, o_ref, acc_ref):
    k = pl.program_id(1)
    last = pl.num_programs(1) - 1

    @pl.when(k == 0)
    def _():
        acc_ref[...] = _dot(a_ref[...], w_ref[...])

    @pl.when((k > 0) & (k < last))
    def _():
        acc_ref[...] += _dot(a_ref[...], w_ref[...])

    @pl.when(k == last)
    def _():
        for r0 in range(0, a_ref.shape[0], EPILOGUE_ROWS):
            rows = slice(r0, r0 + EPILOGUE_ROWS)
            y = acc_ref[rows, :] + _dot(a_ref[rows, :], w_ref[...])
            o_ref[rows, :] = x_ref[rows, :] + _rmsnorm_rows(y, g_ref[...])


def ffn_down(a, w_down, x, g_post):
    m, d = x.shape
    d_ff = a.shape[1]
    tm = _tile(m, 1024)
    tk = 512
    assert d_ff // tk >= 2
    row = lambda: pl.BlockSpec((tm, d), lambda i, k: (i, 0))
    return pl.pallas_call(
        _ffn_down_kernel,
        grid=(m // tm, d_ff // tk),
        in_specs=[pl.BlockSpec((tm, tk), lambda i, k: (i, k)),
                  pl.BlockSpec((tk, d), lambda i, k: (k, 0)),
                  row(), pl.BlockSpec((1, d), lambda i, k: (0, 0))],
        out_specs=row(),
        out_shape=jax.ShapeDtypeStruct((m, d), F32),
        scratch_shapes=[pltpu.VMEM((tm, d), F32)],
        compiler_params=_params(2),
        name="ffn_down",
    )(a, w_down, x, g_post.reshape(1, d))


def kernel(x, mem, g_mix_pre, w_in, w_gate, b_gate, conv_w, w_conv_out, lambda_q1, lambda_k1, lambda_q2, lambda_k2, g_diff_sub, w_diff_out, g_mem, w_mem_kv, w_cross_out, w_o, g_mix_post, g_ffn_pre, w_up, ffn_conv_w, w_down, g_ffn_post):
    bsz, seq, d = x.shape
    mem_len = mem.shape[1]
    depth = w_in.shape[0]
    m = bsz * seq
    q_lo = N_BRANCH * CONV_WIDTH
    qkvx_width = 2 * DIFF_QK_WIDTH + DIFF_WIDTH + CROSS_WIDTH
    qkvx_scale = jnp.where(jnp.arange(qkvx_width) < DIFF_QK_WIDTH,
                           LOG2E * DIFF_QK_DIM ** -0.5, 1.0).astype(F32)
    kv_scale = jnp.ones((2 * CROSS_WIDTH,), F32)

    xf = x.reshape(m, d)
    h = rmsnorm_bf16(xf, g_mix_pre[0])
    for l in range(depth):
        lambda_init = 0.8 - 0.6 * math.exp(-0.3 * l)
        w_in_b = w_in[l].astype(BF16)
        ya = conv_mixer(h, w_in_b, conv_w[l], seq)
        qkvx = project(h, w_in_b, qkvx_scale, q_lo // 1024).reshape(bsz, seq, qkvx_width)

        lam_rows = jnp.stack([lambda_q1[l], lambda_k1[l], lambda_q2[l], lambda_k2[l]])
        yd = diff_attention(qkvx, lam_rows, g_diff_sub[l], bsz, seq, lambda_init)

        mem_n = rmsnorm_bf16(mem.reshape(bsz * mem_len, d), g_mem[l])
        kv = project(mem_n, w_mem_kv[l].astype(BF16), kv_scale, 0, tm_want=bsz * mem_len)
        yx = cross_attention(qkvx, kv.reshape(bsz, mem_len, 2 * CROSS_WIDTH), bsz, seq)

        merged = gated_merge(h, ya, yd.reshape(m, DIFF_WIDTH), yx.reshape(m, CROSS_WIDTH),
                             w_gate[l].astype(BF16), b_gate[l].reshape(1, -1),
                             w_conv_out[l].astype(BF16), w_diff_out[l].astype(BF16),
                             w_cross_out[l].astype(BF16))
        xf, h2 = out_proj_residual(merged, w_o[l].astype(BF16), xf, g_mix_post[l], g_ffn_pre[l])

        act = ffn_up(h2, w_up[l].astype(BF16), ffn_conv_w[l], seq)
        xf = ffn_down(act, w_down[l].astype(BF16), xf, g_ffn_post[l])
        if l + 1 < depth:
            h = rmsnorm_bf16(xf, g_mix_pre[l + 1])
    return xf.reshape(bsz, seq, d)
```

```python
import functools
import math

import jax
import jax.numpy as jnp
from jax import lax
from jax.experimental import pallas as pl
from jax.experimental.pallas import tpu as pltpu

EPS = 1e-6
CHUNK = 64
CONV_WIDTH = 1024
CONV_K = 3
DIFF_HEADS = 8
DIFF_QK_DIM = 64
DIFF_V_DIM = 2 * DIFF_QK_DIM
DIFF_QK_WIDTH = DIFF_HEADS * 2 * DIFF_QK_DIM
DIFF_WIDTH = DIFF_HEADS * DIFF_V_DIM
CROSS_HEADS = 4
CROSS_HEAD_DIM = 256
CROSS_WIDTH = CROSS_HEADS * CROSS_HEAD_DIM
N_BRANCH = 3
LOG2E = math.log2(math.e)
MAX_EXCESS = 30.0
GROUP = 8

SUBLANES = 8
BF16_ROWS = 16
FFN_UP_ROWS = 128
EPILOGUE_ROWS = 256
VMEM_LIMIT = 56 * 1024 * 1024

F32 = jnp.float32
BF16 = jnp.bfloat16


def _params(n_axes):
    return pltpu.CompilerParams(
        dimension_semantics=("arbitrary",) * n_axes, vmem_limit_bytes=VMEM_LIMIT)


def _dot(a, b):
    return jnp.dot(a, b, preferred_element_type=F32)


def _dot_nt(a, b):
    return lax.dot_general(a, b, (((1,), (1,)), ((), ())), preferred_element_type=F32)


def _tile(n, want):
    t = min(n, want)
    assert n % t == 0, (n, want)
    return t


def _rmsnorm_rows(x, g):
    ms = jnp.mean(x * x, axis=-1, keepdims=True)
    return x * lax.rsqrt(ms + EPS) * g


def _rmsnorm_kernel(x_ref, g_ref, o_ref):
    o_ref[...] = _rmsnorm_rows(x_ref[...], g_ref[...]).astype(o_ref.dtype)


def rmsnorm_bf16(x, g):
    m, d = x.shape
    tm = _tile(m, 512)
    return pl.pallas_call(
        _rmsnorm_kernel,
        grid=(m // tm,),
        in_specs=[pl.BlockSpec((tm, d), lambda i: (i, 0)),
                  pl.BlockSpec((1, d), lambda i: (0, 0))],
        out_specs=pl.BlockSpec((tm, d), lambda i: (i, 0)),
        out_shape=jax.ShapeDtypeStruct((m, d), BF16),
        compiler_params=_params(1),
        name="rmsnorm",
    )(x, g.reshape(1, d))


def _shift_rows(u, prev, s):
    rolled = pltpu.roll(u, s, axis=0)
    rprev = pltpu.roll(prev, s, axis=0)
    rows = lax.broadcasted_iota(jnp.int32, prev.shape, 0)
    first = jnp.where(rows < s, rprev, rolled[0:SUBLANES])
    return jnp.concatenate([first, rolled[SUBLANES:]], axis=0)


def _causal_conv3(u, prev, w):
    return (w[2:3] * u + w[1:2] * _shift_rows(u, prev, 1) + w[0:1] * _shift_rows(u, prev, 2))


def _load_carry(carry_ref, j, at_seq_start):
    @pl.when(at_seq_start)
    def _():
        carry_ref[j] = jnp.zeros(carry_ref.shape[1:], carry_ref.dtype)
    return carry_ref[j]


def _conv_mixer_kernel(x_ref, g_ref, wb_ref, wc_ref, wv_ref, cw_ref, o_ref, h_ref, carry_ref, *, tiles_per_seq):
    i = pl.program_id(0)
    j = pl.program_id(1)

    @pl.when(j == 0)
    def _():
        h_ref[...] = _rmsnorm_rows(x_ref[...], g_ref[...]).astype(h_ref.dtype)

    h = h_ref[...]
    u = _dot(h, wc_ref[...]) * _dot(h, wv_ref[...])
    prev = _load_carry(carry_ref, j, i % tiles_per_seq == 0)
    conv = _causal_conv3(u, prev, cw_ref[...])
    carry_ref[j] = u[u.shape[0] - SUBLANES:, :]
    o_ref[...] = (_dot(h, wb_ref[...]) * conv).astype(o_ref.dtype)


def conv_mixer(x, g_pre, w_in, conv_w, seq):
    m, d = x.shape
    tm = _tile(seq, 1024)
    tn = 512
    nj = CONV_WIDTH // tn
    kern = functools.partial(_conv_mixer_kernel, tiles_per_seq=seq // tm)
    return pl.pallas_call(
        kern,
        grid=(m // tm, nj),
        in_specs=[pl.BlockSpec((tm, d), lambda i, j: (i, 0)),
                  pl.BlockSpec((1, d), lambda i, j: (0, 0)),
                  pl.BlockSpec((d, tn), lambda i, j: (0, j)),
                  pl.BlockSpec((d, tn), lambda i, j: (0, nj + j)),
                  pl.BlockSpec((d, tn), lambda i, j: (0, 2 * nj + j)),
                  pl.BlockSpec((CONV_K, tn), lambda i, j: (0, j))],
        out_specs=[pl.BlockSpec((tm, tn), lambda i, j: (i, j)),
                   pl.BlockSpec((tm, d), lambda i, j: (i, 0))],
        out_shape=[jax.ShapeDtypeStruct((m, CONV_WIDTH), BF16), jax.ShapeDtypeStruct((m, d), BF16)],
        scratch_shapes=[pltpu.VMEM((nj, SUBLANES, tn), F32)],
        compiler_params=_params(2),
        name="conv_mixer",
    )(x, g_pre.reshape(1, d), w_in, w_in, w_in, conv_w)


def _proj_kernel(h_ref, w_ref, scale_ref, o_ref):
    o_ref[...] = (_dot(h_ref[...], w_ref[...]) * scale_ref[...]).astype(o_ref.dtype)


def project(h, w, col_scale, col_block0, tm_want=1024, tn=1024):
    m, d = h.shape
    n_out = col_scale.shape[0]
    tm = _tile(m, tm_want)
    return pl.pallas_call(
        _proj_kernel,
        grid=(m // tm, n_out // tn),
        in_specs=[pl.BlockSpec((tm, d), lambda i, j: (i, 0)),
                  pl.BlockSpec((d, tn), lambda i, j: (0, col_block0 + j)),
                  pl.BlockSpec((1, tn), lambda i, j: (0, j))],
        out_specs=pl.BlockSpec((tm, tn), lambda i, j: (i, j)),
        out_shape=jax.ShapeDtypeStruct((m, n_out), BF16),
        compiler_params=_params(2),
        name="project",
    )(h, w, col_scale.reshape(1, n_out))


def _col_max(s):
    r = s.shape[0]
    while r > SUBLANES:
        r //= 2
        s = jnp.maximum(s[:r], s[r:])
    return jnp.max(s, axis=0, keepdims=True)


def _diff_attn_kernel(lam_ref, q_ref, k_ref, v_ref, gsub_ref, o_ref,
                      vt_ref, s_ref, p_ref, alpha_ref, m_ref, excess_ref, acc_ref, *, tq, tk, lambda_init):
    qi = pl.program_id(2)
    seq = k_ref.shape[1]
    dv = DIFF_V_DIM
    assert tq == 2 * tk

    @pl.when(qi == 0)
    def _():
        vt_ref[dv:, :] = jnp.ones((BF16_ROWS, seq), BF16)

        def body(c, carry):
            r = pl.multiple_of(c * tq, tq)
            vt_ref[0:dv, pl.ds(r, tq)] = v_ref[0, pl.ds(r, tq), :].astype(F32).T.astype(BF16)
            return carry
        lax.fori_loop(0, seq // tq, body, 0)

    q = q_ref[0].astype(F32)
    lane = lax.broadcasted_iota(jnp.int32, q.shape, 1)
    qz = (jnp.where(lane < DIFF_QK_DIM, q, 0.0).astype(BF16),
          jnp.where(lane >= DIFF_QK_DIM, q, 0.0).astype(BF16))

    def k_tile(t):
        return k_ref[0, pl.ds(pl.multiple_of(t * tk, tk), tk), :]

    def vt_tile(t):
        return vt_ref[:, pl.ds(pl.multiple_of(t * tk, tk), tk)]

    key_chunk = lax.broadcasted_iota(jnp.int32, (tk, tq), 0) // CHUNK
    qry_chunk = lax.broadcasted_iota(jnp.int32, (tk, tq), 1) // CHUNK
    diag_masks = (key_chunk <= qry_chunk, key_chunk + tk // CHUNK <= qry_chunk)

    def exact_tile(t):
        k_t = k_tile(t)
        vt_t = vt_tile(t)
        for c in range(2):
            s = _dot_nt(k_t, qz[c])
            m_old = m_ref[c]
            m_new = jnp.maximum(m_old, _col_max(s))
            p = jnp.exp2(s - m_new).astype(BF16)
            acc_ref[c] = jnp.exp2(m_old - m_new) * acc_ref[c] + _dot(vt_t, p)
            m_ref[c] = m_new

    def exact_diagonal():
        k_d = [k_tile(2 * qi + d) for d in range(2)]
        vt_d = [vt_tile(2 * qi + d) for d in range(2)]
        for c in range(2):
            s = [jnp.where(diag_masks[d], _dot_nt(k_d[d], qz[c]), -jnp.inf) for d in range(2)]
            m = jnp.maximum(_col_max(s[0]), _col_max(s[1]))
            acc_ref[c] = (_dot(vt_d[0], jnp.exp2(s[0] - m).astype(BF16))
                          + _dot(vt_d[1], jnp.exp2(s[1] - m).astype(BF16)))
            m_ref[c] = m

    def fast_group(t0, n, masks=None):
        assert 2 <= n <= GROUP
        slots = range(GROUP - n, GROUP)
        m_frozen = [m_ref[c] for c in range(2)]
        tile_max = [None, None]
        if masks is None:
            k_next = k_tile(t0 + n)
            s_next = [_dot_nt(k_next, qz[c]) for c in range(2)]
        for c in range(2):
            s = s_ref[c]
            if masks is not None:
                s = jnp.where(masks[0], s, -jnp.inf)
            tile_max[c] = _col_max(s)
            p_ref[slots[0], c] = jnp.exp2(s - m_frozen[c]).astype(BF16)
            if masks is None:
                s_ref[c] = s_next[c]
        vt_pending = vt_tile(jnp.maximum(t0 - 1, 0))
        acc = [alpha_ref[c] * (acc_ref[c] + _dot(vt_pending, p_ref[GROUP - 1, c])) for c in range(2)]
        for i in range(1, n):
            k_t = k_tile(t0 + i)
            for c in range(2):
                s = _dot_nt(k_t, qz[c])
                if masks is not None:
                    s = jnp.where(masks[i], s, -jnp.inf)
                tile_max[c] = jnp.maximum(tile_max[c], _col_max(s))
                p_ref[slots[i], c] = jnp.exp2(s - m_frozen[c]).astype(BF16)
            vt_t = vt_tile(t0 + i - 1)
            for c in range(2):
                acc[c] = acc[c] + _dot(vt_t, p_ref[slots[i - 1], c])
        if masks is not None:
            vt_t = vt_tile(t0 + n - 1)
            for c in range(2):
                acc[c] = acc[c] + _dot(vt_t, p_ref[slots[n - 1], c])
        for c in range(2):
            acc_ref[c] = acc[c]
            m_new = jnp.maximum(m_frozen[c], tile_max[c])
            alpha_ref[c] = jnp.exp2(m_frozen[c] - m_new)
            excess_ref[c] = jnp.maximum(excess_ref[c], tile_max[c] - m_frozen[c])
            m_ref[c] = m_new

    for c in range(2):
        s0 = _dot_nt(k_tile(0), qz[c])
        s_ref[c] = s0
        m_ref[c] = _col_max(s0[0:CHUNK])
    acc_ref[...] = jnp.zeros(acc_ref.shape, F32)
    p_ref[GROUP - 1] = jnp.zeros(p_ref.shape[1:], BF16)
    alpha_ref[...] = jnp.ones(alpha_ref.shape, F32)
    excess_ref[...] = jnp.zeros(excess_ref.shape, F32)
    n_tiles = 2 * qi
    rem = n_tiles % GROUP

    @pl.when(rem % 4 > 0)
    def _():
        fast_group(0, 2)

    @pl.when(rem >= 4)
    def _():
        fast_group(rem % 4, 4)

    def group(j, carry):
        fast_group(rem + GROUP * j, GROUP)
        return carry
    lax.fori_loop(0, n_tiles // GROUP, group, 0)
    fast_group(n_tiles, 2, diag_masks)

    worst = jnp.max(jnp.maximum(excess_ref[0], excess_ref[1]), axis=-1, keepdims=True)

    @pl.when(worst[0, 0] > MAX_EXCESS)
    def _():
        exact_diagonal()

        def body(t, carry):
            exact_tile(t)
            return carry
        lax.fori_loop(0, 2 * qi, body, 0)

    lv = lam_ref[...]
    lam = (jnp.exp(jnp.sum(lv[0:1] * lv[1:2], axis=-1, keepdims=True))
           - jnp.exp(jnp.sum(lv[2:3] * lv[3:4], axis=-1, keepdims=True)) + lambda_init)
    a1 = acc_ref[0]
    a2 = acc_ref[1]
    o = a1[0:dv] / a1[dv:dv + 1] - lam * (a2[0:dv] / a2[dv:dv + 1])
    ms = jnp.mean(o * o, axis=0, keepdims=True)
    y = o * lax.rsqrt(ms + EPS) * gsub_ref[...]
    y = y * (1.0 - lambda_init)
    o_ref[0] = y.T.astype(o_ref.dtype)


def diff_attention(qkvx, lam_rows, g_sub, bsz, seq, lambda_init):
    tq = _tile(seq, 512)
    tk = tq // 2
    nq = seq // tq
    dv = DIFF_V_DIM
    kern = functools.partial(_diff_attn_kernel, tq=tq, tk=tk, lambda_init=lambda_init)
    return pl.pallas_call(
        kern,
        grid=(bsz, DIFF_HEADS, nq),
        in_specs=[pl.BlockSpec((4, DIFF_QK_DIM), lambda b, h, i: (0, 0)),
                  pl.BlockSpec((1, tq, dv), lambda b, h, i: (b, i, h)),
                  pl.BlockSpec((1, seq, dv), lambda b, h, i: (b, 0, DIFF_HEADS + h)),
                  pl.BlockSpec((1, seq, dv), lambda b, h, i: (b, 0, 2 * DIFF_HEADS + h)),
                  pl.BlockSpec((dv, 1), lambda b, h, i: (0, 0))],
        out_specs=pl.BlockSpec((1, tq, dv), lambda b, h, i: (b, i, h)),
        out_shape=jax.ShapeDtypeStruct((bsz, seq, DIFF_WIDTH), BF16),
        scratch_shapes=[pltpu.VMEM((dv + BF16_ROWS, seq), BF16),
                        pltpu.VMEM((2, tk, tq), F32),
                        pltpu.VMEM((GROUP, 2, tk, tq), BF16),
                        pltpu.VMEM((2, 1, tq), F32),
                        pltpu.VMEM((2, 1, tq), F32),
                        pltpu.VMEM((2, 1, tq), F32),
                        pltpu.VMEM((2, dv + BF16_ROWS, tq), F32)],
        compiler_params=_params(3),
        name="diff_attention",
    )(lam_rows, qkvx, qkvx, qkvx, g_sub.reshape(dv, 1))


def _cross_attn_kernel(xq_ref, kv_ref, o_ref):
    hd = CROSS_HEAD_DIM
    for h in range(CROSS_HEADS):
        q = xq_ref[0, :, h * hd:(h + 1) * hd]
        k = kv_ref[0, :, h * hd:(h + 1) * hd]
        v = kv_ref[0, :, CROSS_WIDTH + h * hd:CROSS_WIDTH + (h + 1) * hd]
        s = _dot_nt(q, k) * (hd ** -0.5)
        p = jnp.exp(s - jnp.max(s, axis=-1, keepdims=True))
        l = jnp.sum(p, axis=-1, keepdims=True)
        o_ref[0, :, h * hd:(h + 1) * hd] = (_dot(p.astype(BF16), v) / l).astype(o_ref.dtype)


def cross_attention(qkvx, kv, bsz, seq):
    tm = _tile(seq, 512)
    mem_len = kv.shape[1]
    return pl.pallas_call(
        _cross_attn_kernel,
        grid=(bsz, seq // tm),
        in_specs=[pl.BlockSpec((1, tm, CROSS_WIDTH), lambda b, i: (b, i, 3)),
                  pl.BlockSpec((1, mem_len, 2 * CROSS_WIDTH), lambda b, i: (b, 0, 0))],
        out_specs=pl.BlockSpec((1, tm, CROSS_WIDTH), lambda b, i: (b, i, 0)),
        out_shape=jax.ShapeDtypeStruct((bsz, seq, CROSS_WIDTH), BF16),
        compiler_params=_params(2),
        name="cross_attention",
    )(qkvx, kv)


def _merge_kernel(h_ref, ya_ref, yd_ref, yx_ref, wga_ref, wgb_ref, wgc_ref,
                  ba_ref, bb_ref, bc_ref, wa_ref, wd_ref, wx_ref, o_ref):
    h = h_ref[...]
    merged = jax.nn.sigmoid(_dot(h, wga_ref[...]) + ba_ref[...]) * _dot(ya_ref[...], wa_ref[...])
    merged += jax.nn.sigmoid(_dot(h, wgb_ref[...]) + bb_ref[...]) * _dot(yd_ref[...], wd_ref[...])
    merged += jax.nn.sigmoid(_dot(h, wgc_ref[...]) + bc_ref[...]) * _dot(yx_ref[...], wx_ref[...])
    o_ref[...] = merged.astype(o_ref.dtype)


def gated_merge(h, ya, yd, yx, w_gate, b_gate, w_a, w_d, w_x):
    m, d = h.shape
    tm = _tile(m, 1024)
    tn = 512
    nj = d // tn
    act = lambda width: pl.BlockSpec((tm, width), lambda i, j: (i, 0))
    gate_w = lambda br: pl.BlockSpec((d, tn), lambda i, j: (0, br * nj + j))
    gate_b = lambda br: pl.BlockSpec((1, tn), lambda i, j: (0, br * nj + j))
    out_w = lambda width: pl.BlockSpec((width, tn), lambda i, j: (0, j))
    return pl.pallas_call(
        _merge_kernel,
        grid=(m // tm, nj),
        in_specs=[act(d), act(CONV_WIDTH), act(DIFF_WIDTH), act(CROSS_WIDTH),
                  gate_w(0), gate_w(1), gate_w(2), gate_b(0), gate_b(1), gate_b(2),
                  out_w(CONV_WIDTH), out_w(DIFF_WIDTH), out_w(CROSS_WIDTH)],
        out_specs=pl.BlockSpec((tm, tn), lambda i, j: (i, j)),
        out_shape=jax.ShapeDtypeStruct((m, d), BF16),
        compiler_params=_params(2),
        name="gated_merge",
    )(h, ya, yd, yx, w_gate, w_gate, w_gate, b_gate, b_gate, b_gate, w_a, w_d, w_x)


def _out_proj_kernel(z_ref, w_ref, x_ref, gpost_ref, gpre_ref, x1_ref, h2_ref):
    for r0 in range(0, z_ref.shape[0], EPILOGUE_ROWS):
        rows = slice(r0, r0 + EPILOGUE_ROWS)
        y = _dot(z_ref[rows, :], w_ref[...])
        x1 = x_ref[rows, :] + _rmsnorm_rows(y, gpost_ref[...])
        x1_ref[rows, :] = x1
        h2_ref[rows, :] = _rmsnorm_rows(x1, gpre_ref[...]).astype(h2_ref.dtype)


def out_proj_residual(z, w_o, x, g_post, g_next_pre):
    m, d = x.shape
    tm = _tile(m, 512)
    row = lambda: pl.BlockSpec((tm, d), lambda i: (i, 0))
    vec = lambda: pl.BlockSpec((1, d), lambda i: (0, 0))
    return pl.pallas_call(
        _out_proj_kernel,
        grid=(m // tm,),
        in_specs=[row(), pl.BlockSpec((d, d), lambda i: (0, 0)), row(), vec(), vec()],
        out_specs=[row(), row()],
        out_shape=[jax.ShapeDtypeStruct((m, d), F32), jax.ShapeDtypeStruct((m, d), BF16)],
        compiler_params=_params(1),
        name="out_proj_residual",
    )(z, w_o, x, g_post.reshape(1, d), g_next_pre.reshape(1, d))


def _ffn_up_kernel(h_ref, wg_ref, wv_ref, cwg_ref, cwv_ref, o_ref, carry_g, carry_v, *, tiles_per_seq):
    i = pl.program_id(0)
    j = pl.program_id(1)
    start = i % tiles_per_seq == 0
    prev_g = _load_carry(carry_g, j, start)
    prev_v = _load_carry(carry_v, j, start)
    for r0 in range(0, h_ref.shape[0], FFN_UP_ROWS):
        h = h_ref[r0:r0 + FFN_UP_ROWS, :]
        zg = _dot(h, wg_ref[...])
        zv = _dot(h, wv_ref[...])
        gate = _causal_conv3(zg, prev_g, cwg_ref[...])
        val = _causal_conv3(zv, prev_v, cwv_ref[...])
        o_ref[r0:r0 + FFN_UP_ROWS, :] = (gate * jax.nn.sigmoid(gate) * val).astype(o_ref.dtype)
        prev_g = zg[FFN_UP_ROWS - SUBLANES:, :]
        prev_v = zv[FFN_UP_ROWS - SUBLANES:, :]
    carry_g[j] = prev_g
    carry_v[j] = prev_v


def ffn_up(h, w_up, conv_w, seq):
    m, d = h.shape
    d_ff = w_up.shape[1] // 2
    tm = _tile(seq, 1024)
    tn = 512
    nj = d_ff // tn
    kern = functools.partial(_ffn_up_kernel, tiles_per_seq=seq // tm)
    return pl.pallas_call(
        kern,
        grid=(m // tm, nj),
        in_specs=[pl.BlockSpec((tm, d), lambda i, j: (i, 0)),
                  pl.BlockSpec((d, tn), lambda i, j: (0, j)),
                  pl.BlockSpec((d, tn), lambda i, j: (0, nj + j)),
                  pl.BlockSpec((CONV_K, tn), lambda i, j: (0, j)),
                  pl.BlockSpec((CONV_K, tn), lambda i, j: (0, nj + j))],
        out_specs=pl.BlockSpec((tm, tn), lambda i, j: (i, j)),
        out_shape=jax.ShapeDtypeStruct((m, d_ff), BF16),
        scratch_shapes=[pltpu.VMEM((nj, SUBLANES, tn), F32), pltpu.VMEM((nj, SUBLANES, tn), F32)],
        compiler_params=_params(2),
        name="ffn_up",
    )(h, w_up, w_up, conv_w, conv_w)


def _ffn_down_kernel(a_ref, w_ref, x_ref, g_ref, o_ref, acc_ref):
    k = pl.program_id(1)
    last = pl.num_programs(1) - 1

    @pl.when(k == 0)
    def _():
        acc_ref[...] = _dot(a_ref[...], w_ref[...])

    @pl.when((k > 0) & (k < last))
    def _():
        acc_ref[...] += _dot(a_ref[...], w_ref[...])

    @pl.when(k == last)
    def _():
        for r0 in range(0, a_ref.shape[0], EPILOGUE_ROWS):
            rows = slice(r0, r0 + EPILOGUE_ROWS)
            y = acc_ref[rows, :] + _dot(a_ref[rows, :], w_ref[...])
            o_ref[rows, :] = x_ref[rows, :] + _rmsnorm_rows(y, g_ref[...])


def ffn_down(a, w_down, x, g_post):
    m, d = x.shape
    d_ff = a.shape[1]
    tm = _tile(m, 1024)
    tk = 512
    assert d_ff // tk >= 2
    row = lambda: pl.BlockSpec((tm, d), lambda i, k: (i, 0))
    return pl.pallas_call(
        _ffn_down_kernel,
        grid=(m // tm, d_ff // tk),
        in_specs=[pl.BlockSpec((tm, tk), lambda i, k: (i, k)),
                  pl.BlockSpec((tk, d), lambda i, k: (k, 0)),
                  row(), pl.BlockSpec((1, d), lambda i, k: (0, 0))],
        out_specs=row(),
        out_shape=jax.ShapeDtypeStruct((m, d), F32),
        scratch_shapes=[pltpu.VMEM((tm, d), F32)],
        compiler_params=_params(2),
        name="ffn_down",
    )(a, w_down, x, g_post.reshape(1, d))


def kernel(x, mem, g_mix_pre, w_in, w_gate, b_gate, conv_w, w_conv_out, lambda_q1, lambda_k1, lambda_q2, lambda_k2, g_diff_sub, w_diff_out, g_mem, w_mem_kv, w_cross_out, w_o, g_mix_post, g_ffn_pre, w_up, ffn_conv_w, w_down, g_ffn_post):
    bsz, seq, d = x.shape
    mem_len = mem.shape[1]
    depth = w_in.shape[0]
    m = bsz * seq
    q_lo = N_BRANCH * CONV_WIDTH
    qkvx_width = 2 * DIFF_QK_WIDTH + DIFF_WIDTH + CROSS_WIDTH
    qkvx_scale = jnp.where(jnp.arange(qkvx_width) < DIFF_QK_WIDTH,
                           LOG2E * DIFF_QK_DIM ** -0.5, 1.0).astype(F32)
    kv_scale = jnp.ones((2 * CROSS_WIDTH,), F32)

    xf = x.reshape(m, d)
    for l in range(depth):
        lambda_init = 0.8 - 0.6 * math.exp(-0.3 * l)
        w_in_b = w_in[l].astype(BF16)
        ya, h = conv_mixer(xf, g_mix_pre[l], w_in_b, conv_w[l], seq)
        qkvx = project(h, w_in_b, qkvx_scale, q_lo // 1024).reshape(bsz, seq, qkvx_width)

        lam_rows = jnp.stack([lambda_q1[l], lambda_k1[l], lambda_q2[l], lambda_k2[l]])
        yd = diff_attention(qkvx, lam_rows, g_diff_sub[l], bsz, seq, lambda_init)

        mem_n = rmsnorm_bf16(mem.reshape(bsz * mem_len, d), g_mem[l])
        kv = project(mem_n, w_mem_kv[l].astype(BF16), kv_scale, 0, tm_want=bsz * mem_len)
        yx = cross_attention(qkvx, kv.reshape(bsz, mem_len, 2 * CROSS_WIDTH), bsz, seq)

        merged = gated_merge(h, ya, yd.reshape(m, DIFF_WIDTH), yx.reshape(m, CROSS_WIDTH),
                             w_gate[l].astype(BF16), b_gate[l].reshape(1, -1),
                             w_conv_out[l].astype(BF16), w_diff_out[l].astype(BF16),
                             w_cross_out[l].astype(BF16))
        xf, h2 = out_proj_residual(merged, w_o[l].astype(BF16), xf, g_mix_post[l], g_ffn_pre[l])

        act = ffn_up(h2, w_up[l].astype(BF16), ffn_conv_w[l], seq)
        xf = ffn_down(act, w_down[l].astype(BF16), xf, g_ffn_post[l])
    return xf.reshape(bsz, seq, d)
```

```python
import functools
import math

import jax
import jax.numpy as jnp
from jax import lax
from jax.experimental import pallas as pl
from jax.experimental.pallas import tpu as pltpu

EPS = 1e-6
CHUNK = 64
CONV_WIDTH = 1024
CONV_K = 3
DIFF_HEADS = 8
DIFF_QK_DIM = 64
DIFF_V_DIM = 2 * DIFF_QK_DIM
DIFF_QK_WIDTH = DIFF_HEADS * 2 * DIFF_QK_DIM
DIFF_WIDTH = DIFF_HEADS * DIFF_V_DIM
CROSS_HEADS = 4
CROSS_HEAD_DIM = 256
CROSS_WIDTH = CROSS_HEADS * CROSS_HEAD_DIM
N_BRANCH = 3
LOG2E = math.log2(math.e)
MAX_EXCESS = 30.0
GROUP = 8

SUBLANES = 8
BF16_ROWS = 16
FFN_UP_ROWS = 128
EPILOGUE_ROWS = 256
VMEM_LIMIT = 56 * 1024 * 1024

F32 = jnp.float32
BF16 = jnp.bfloat16


def _params(n_axes):
    return pltpu.CompilerParams(
        dimension_semantics=("arbitrary",) * n_axes, vmem_limit_bytes=VMEM_LIMIT)


def _dot(a, b):
    return lax.dot_general(a, b, (((1,), (0,)), ((), ())), preferred_element_type=F32)


def _dot_nt(a, b):
    return lax.dot_general(a, b, (((1,), (1,)), ((), ())), preferred_element_type=F32)


def _tile(n, want):
    t = min(n, want)
    assert n % t == 0, (n, want)
    return t


def _rmsnorm_rows(x, g):
    ms = jnp.mean(x * x, axis=-1, keepdims=True)
    return x * lax.rsqrt(ms + EPS) * g


def _rmsnorm_kernel(x_ref, g_ref, o_ref):
    o_ref[...] = _rmsnorm_rows(x_ref[...], g_ref[...]).astype(o_ref.dtype)


def rmsnorm_bf16(x, g):
    m, d = x.shape
    tm = _tile(m, 512)
    return pl.pallas_call(
        _rmsnorm_kernel,
        grid=(m // tm,),
        in_specs=[pl.BlockSpec((tm, d), lambda i: (i, 0)),
                  pl.BlockSpec((1, d), lambda i: (0, 0))],
        out_specs=pl.BlockSpec((tm, d), lambda i: (i, 0)),
        out_shape=jax.ShapeDtypeStruct((m, d), BF16),
        compiler_params=_params(1),
        name="rmsnorm",
    )(x, g.reshape(1, d))


def _shift_rows(u, prev, s):
    rolled = pltpu.roll(u, s, axis=0)
    rprev = pltpu.roll(prev, s, axis=0)
    rows = lax.broadcasted_iota(jnp.int32, prev.shape, 0)
    first = jnp.where(rows < s, rprev, rolled[0:SUBLANES])
    return jnp.concatenate([first, rolled[SUBLANES:]], axis=0)


def _causal_conv3(u, prev, w):
    return (w[2:3] * u + w[1:2] * _shift_rows(u, prev, 1) + w[0:1] * _shift_rows(u, prev, 2))


def _load_carry(carry_ref, j, at_seq_start):
    @pl.when(at_seq_start)
    def _():
        carry_ref[j] = jnp.zeros(carry_ref.shape[1:], carry_ref.dtype)
    return carry_ref[j]


def _conv_mixer_kernel(x_ref, g_ref, wb_ref, wc_ref, wv_ref, cw_ref, o_ref, h_ref, carry_ref, *, tiles_per_seq):
    i = pl.program_id(0)
    j = pl.program_id(1)

    @pl.when(j == 0)
    def _():
        h_ref[...] = _rmsnorm_rows(x_ref[...], g_ref[...]).astype(h_ref.dtype)

    h = h_ref[...]
    u = _dot(h, wc_ref[...]) * _dot(h, wv_ref[...])
    prev = _load_carry(carry_ref, j, i % tiles_per_seq == 0)
    conv = _causal_conv3(u, prev, cw_ref[...])
    carry_ref[j] = u[u.shape[0] - SUBLANES:, :]
    o_ref[...] = (_dot(h, wb_ref[...]) * conv).astype(o_ref.dtype)


def conv_mixer(x, g_pre, w_in, conv_w, seq):
    m, d = x.shape
    tm = _tile(seq, 1024)
    tn = 512
    nj = CONV_WIDTH // tn
    kern = functools.partial(_conv_mixer_kernel, tiles_per_seq=seq // tm)
    return pl.pallas_call(
        kern,
        grid=(m // tm, nj),
        in_specs=[pl.BlockSpec((tm, d), lambda i, j: (i, 0)),
                  pl.BlockSpec((1, d), lambda i, j: (0, 0)),
                  pl.BlockSpec((d, tn), lambda i, j: (0, j)),
                  pl.BlockSpec((d, tn), lambda i, j: (0, nj + j)),
                  pl.BlockSpec((d, tn), lambda i, j: (0, 2 * nj + j)),
                  pl.BlockSpec((CONV_K, tn), lambda i, j: (0, j))],
        out_specs=[pl.BlockSpec((tm, tn), lambda i, j: (i, j)),
                   pl.BlockSpec((tm, d), lambda i, j: (i, 0))],
        out_shape=[jax.ShapeDtypeStruct((m, CONV_WIDTH), BF16), jax.ShapeDtypeStruct((m, d), BF16)],
        scratch_shapes=[pltpu.VMEM((nj, SUBLANES, tn), F32)],
        compiler_params=_params(2),
        name="conv_mixer",
    )(x, g_pre.reshape(1, d), w_in, w_in, w_in, conv_w)


def _proj_kernel(h_ref, w_ref, scale_ref, o_ref):
    o_ref[...] = (_dot(h_ref[...], w_ref[...]) * scale_ref[...]).astype(o_ref.dtype)


def project(h, w, col_scale, col_block0, tm_want=1024, tn=1024):
    m, d = h.shape
    n_out = col_scale.shape[0]
    tm = _tile(m, tm_want)
    return pl.pallas_call(
        _proj_kernel,
        grid=(m // tm, n_out // tn),
        in_specs=[pl.BlockSpec((tm, d), lambda i, j: (i, 0)),
                  pl.BlockSpec((d, tn), lambda i, j: (0, col_block0 + j)),
                  pl.BlockSpec((1, tn), lambda i, j: (0, j))],
        out_specs=pl.BlockSpec((tm, tn), lambda i, j: (i, j)),
        out_shape=jax.ShapeDtypeStruct((m, n_out), BF16),
        compiler_params=_params(2),
        name="project",
    )(h, w, col_scale.reshape(1, n_out))


def _col_max(s):
    r = s.shape[0]
    while r > SUBLANES:
        r //= 2
        s = jnp.maximum(s[:r], s[r:])
    return jnp.max(s, axis=0, keepdims=True)


def _diff_attn_kernel(lam_ref, q_ref, k_ref, v_ref, gsub_ref, o_ref,
                      vt_ref, s_ref, p_ref, alpha_ref, m_ref, excess_ref, acc_ref, *, tq, tk, lambda_init):
    qi = pl.program_id(2)
    seq = k_ref.shape[1]
    dv = DIFF_V_DIM
    assert tq == 2 * tk

    @pl.when(qi == 0)
    def _():
        vt_ref[dv:, :] = jnp.ones((BF16_ROWS, seq), BF16)

        def body(c, carry):
            r = pl.multiple_of(c * tq, tq)
            vt_ref[0:dv, pl.ds(r, tq)] = v_ref[0, pl.ds(r, tq), :].astype(F32).T.astype(BF16)
            return carry
        lax.fori_loop(0, seq // tq, body, 0)

    q = q_ref[0].astype(F32)
    lane = lax.broadcasted_iota(jnp.int32, q.shape, 1)
    qz = (jnp.where(lane < DIFF_QK_DIM, q, 0.0).astype(BF16),
          jnp.where(lane >= DIFF_QK_DIM, q, 0.0).astype(BF16))

    def k_tile(t):
        return k_ref[0, pl.ds(pl.multiple_of(t * tk, tk), tk), :]

    def vt_tile(t):
        return vt_ref[:, pl.ds(pl.multiple_of(t * tk, tk), tk)]

    key_chunk = lax.broadcasted_iota(jnp.int32, (tk, tq), 0) // CHUNK
    qry_chunk = lax.broadcasted_iota(jnp.int32, (tk, tq), 1) // CHUNK
    diag_masks = (key_chunk <= qry_chunk, key_chunk + tk // CHUNK <= qry_chunk)

    def exact_tile(t):
        k_t = k_tile(t)
        vt_t = vt_tile(t)
        for c in range(2):
            s = _dot_nt(k_t, qz[c])
            m_old = m_ref[c]
            m_new = jnp.maximum(m_old, _col_max(s))
            p = jnp.exp2(s - m_new).astype(BF16)
            acc_ref[c] = jnp.exp2(m_old - m_new) * acc_ref[c] + _dot(vt_t, p)
            m_ref[c] = m_new

    def exact_diagonal():
        k_d = [k_tile(2 * qi + d) for d in range(2)]
        vt_d = [vt_tile(2 * qi + d) for d in range(2)]
        for c in range(2):
            s = [jnp.where(diag_masks[d], _dot_nt(k_d[d], qz[c]), -jnp.inf) for d in range(2)]
            m = jnp.maximum(_col_max(s[0]), _col_max(s[1]))
            acc_ref[c] = (_dot(vt_d[0], jnp.exp2(s[0] - m).astype(BF16))
                          + _dot(vt_d[1], jnp.exp2(s[1] - m).astype(BF16)))
            m_ref[c] = m

    def fast_group(t0, n, masks=None):
        assert 2 <= n <= GROUP
        slots = range(GROUP - n, GROUP)
        m_frozen = [m_ref[c] for c in range(2)]
        tile_max = [None, None]
        if masks is None:
            k_next = k_tile(t0 + n)
            s_next = [_dot_nt(k_next, qz[c]) for c in range(2)]
        for c in range(2):
            s = s_ref[c]
            if masks is not None:
                s = jnp.where(masks[0], s, -jnp.inf)
            tile_max[c] = _col_max(s)
            p_ref[slots[0], c] = jnp.exp2(s - m_frozen[c]).astype(BF16)
            if masks is None:
                s_ref[c] = s_next[c]
        vt_pending = vt_tile(jnp.maximum(t0 - 1, 0))
        acc = [alpha_ref[c] * (acc_ref[c] + _dot(vt_pending, p_ref[GROUP - 1, c])) for c in range(2)]
        for i in range(1, n):
            k_t = k_tile(t0 + i)
            for c in range(2):
                s = _dot_nt(k_t, qz[c])
                if masks is not None:
                    s = jnp.where(masks[i], s, -jnp.inf)
                tile_max[c] = jnp.maximum(tile_max[c], _col_max(s))
                p_ref[slots[i], c] = jnp.exp2(s - m_frozen[c]).astype(BF16)
            vt_t = vt_tile(t0 + i - 1)
            for c in range(2):
                acc[c] = acc[c] + _dot(vt_t, p_ref[slots[i - 1], c])
        if masks is not None:
            vt_t = vt_tile(t0 + n - 1)
            for c in range(2):
                acc[c] = acc[c] + _dot(vt_t, p_ref[slots[n - 1], c])
        for c in range(2):
            acc_ref[c] = acc[c]
            m_new = jnp.maximum(m_frozen[c], tile_max[c])
            alpha_ref[c] = jnp.exp2(m_frozen[c] - m_new)
            excess_ref[c] = jnp.maximum(excess_ref[c], tile_max[c] - m_frozen[c])
            m_ref[c] = m_new

    for c in range(2):
        s0 = _dot_nt(k_tile(0), qz[c])
        s_ref[c] = s0
        m_ref[c] = _col_max(s0[0:CHUNK])
    acc_ref[...] = jnp.zeros(acc_ref.shape, F32)
    p_ref[GROUP - 1] = jnp.zeros(p_ref.shape[1:], BF16)
    alpha_ref[...] = jnp.ones(alpha_ref.shape, F32)
    excess_ref[...] = jnp.zeros(excess_ref.shape, F32)
    n_tiles = 2 * qi
    rem = n_tiles % GROUP

    @pl.when(rem % 4 > 0)
    def _():
        fast_group(0, 2)

    @pl.when(rem >= 4)
    def _():
        fast_group(rem % 4, 4)

    def group(j, carry):
        fast_group(rem + GROUP * j, GROUP)
        return carry
    lax.fori_loop(0, n_tiles // GROUP, group, 0)
    fast_group(n_tiles, 2, diag_masks)

    worst = jnp.max(jnp.maximum(excess_ref[0], excess_ref[1]), axis=-1, keepdims=True)

    @pl.when(worst[0, 0] > MAX_EXCESS)
    def _():
        exact_diagonal()

        def body(t, carry):
            exact_tile(t)
            return carry
        lax.fori_loop(0, 2 * qi, body, 0)

    lv = lam_ref[...]
    lam = (jnp.exp(jnp.sum(lv[0:1] * lv[1:2], axis=-1, keepdims=True))
           - jnp.exp(jnp.sum(lv[2:3] * lv[3:4], axis=-1, keepdims=True)) + lambda_init)
    a1 = acc_ref[0]
    a2 = acc_ref[1]
    o = a1[0:dv] / a1[dv:dv + 1] - lam * (a2[0:dv] / a2[dv:dv + 1])
    ms = jnp.mean(o * o, axis=0, keepdims=True)
    y = o * lax.rsqrt(ms + EPS) * gsub_ref[...]
    y = y * (1.0 - lambda_init)
    o_ref[0] = y.T.astype(o_ref.dtype)


def diff_attention(qkvx, lam_rows, g_sub, bsz, seq, lambda_init):
    tq = _tile(seq, 512)
    tk = tq // 2
    nq = seq // tq
    dv = DIFF_V_DIM
    kern = functools.partial(_diff_attn_kernel, tq=tq, tk=tk, lambda_init=lambda_init)
    return pl.pallas_call(
        kern,
        grid=(bsz, DIFF_HEADS, nq),
        in_specs=[pl.BlockSpec((4, DIFF_QK_DIM), lambda b, h, i: (0, 0)),
                  pl.BlockSpec((1, tq, dv), lambda b, h, i: (b, i, h)),
                  pl.BlockSpec((1, seq, dv), lambda b, h, i: (b, 0, DIFF_HEADS + h)),
                  pl.BlockSpec((1, seq, dv), lambda b, h, i: (b, 0, 2 * DIFF_HEADS + h)),
                  pl.BlockSpec((dv, 1), lambda b, h, i: (0, 0))],
        out_specs=pl.BlockSpec((1, tq, dv), lambda b, h, i: (b, i, h)),
        out_shape=jax.ShapeDtypeStruct((bsz, seq, DIFF_WIDTH), BF16),
        scratch_shapes=[pltpu.VMEM((dv + BF16_ROWS, seq), BF16),
                        pltpu.VMEM((2, tk, tq), F32),
                        pltpu.VMEM((GROUP, 2, tk, tq), BF16),
                        pltpu.VMEM((2, 1, tq), F32),
                        pltpu.VMEM((2, 1, tq), F32),
                        pltpu.VMEM((2, 1, tq), F32),
                        pltpu.VMEM((2, dv + BF16_ROWS, tq), F32)],
        compiler_params=_params(3),
        name="diff_attention",
    )(lam_rows, qkvx, qkvx, qkvx, g_sub.reshape(dv, 1))


def _cross_attn_kernel(xq_ref, kv_ref, o_ref):
    hd = CROSS_HEAD_DIM
    for h in range(CROSS_HEADS):
        q = xq_ref[0, :, h * hd:(h + 1) * hd]
        k = kv_ref[0, :, h * hd:(h + 1) * hd]
        v = kv_ref[0, :, CROSS_WIDTH + h * hd:CROSS_WIDTH + (h + 1) * hd]
        s = _dot_nt(q, k) * (hd ** -0.5)
        p = jnp.exp(s - jnp.max(s, axis=-1, keepdims=True))
        l = jnp.sum(p, axis=-1, keepdims=True)
        o_ref[0, :, h * hd:(h + 1) * hd] = (_dot(p.astype(BF16), v) / l).astype(o_ref.dtype)


def cross_attention(qkvx, kv, bsz, seq):
    tm = _tile(seq, 512)
    mem_len = kv.shape[1]
    return pl.pallas_call(
        _cross_attn_kernel,
        grid=(bsz, seq // tm),
        in_specs=[pl.BlockSpec((1, tm, CROSS_WIDTH), lambda b, i: (b, i, 3)),
                  pl.BlockSpec((1, mem_len, 2 * CROSS_WIDTH), lambda b, i: (b, 0, 0))],
        out_specs=pl.BlockSpec((1, tm, CROSS_WIDTH), lambda b, i: (b, i, 0)),
        out_shape=jax.ShapeDtypeStruct((bsz, seq, CROSS_WIDTH), BF16),
        compiler_params=_params(2),
        name="cross_attention",
    )(qkvx, kv)


def _merge_kernel(h_ref, ya_ref, yd_ref, yx_ref, wga_ref, wgb_ref, wgc_ref,
                  ba_ref, bb_ref, bc_ref, wa_ref, wd_ref, wx_ref, o_ref):
    h = h_ref[...]
    merged = jax.nn.sigmoid(_dot(h, wga_ref[...]) + ba_ref[...]) * _dot(ya_ref[...], wa_ref[...])
    merged += jax.nn.sigmoid(_dot(h, wgb_ref[...]) + bb_ref[...]) * _dot(yd_ref[...], wd_ref[...])
    merged += jax.nn.sigmoid(_dot(h, wgc_ref[...]) + bc_ref[...]) * _dot(yx_ref[...], wx_ref[...])
    o_ref[...] = merged.astype(o_ref.dtype)


def gated_merge(h, ya, yd, yx, w_gate, b_gate, w_a, w_d, w_x):
    m, d = h.shape
    tm = _tile(m, 1024)
    tn = 512
    nj = d // tn
    act = lambda width: pl.BlockSpec((tm, width), lambda i, j: (i, 0))
    gate_w = lambda br: pl.BlockSpec((d, tn), lambda i, j: (0, br * nj + j))
    gate_b = lambda br: pl.BlockSpec((1, tn), lambda i, j: (0, br * nj + j))
    out_w = lambda width: pl.BlockSpec((width, tn), lambda i, j: (0, j))
    return pl.pallas_call(
        _merge_kernel,
        grid=(m // tm, nj),
        in_specs=[act(d), act(CONV_WIDTH), act(DIFF_WIDTH), act(CROSS_WIDTH),
                  gate_w(0), gate_w(1), gate_w(2), gate_b(0), gate_b(1), gate_b(2),
                  out_w(CONV_WIDTH), out_w(DIFF_WIDTH), out_w(CROSS_WIDTH)],
        out_specs=pl.BlockSpec((tm, tn), lambda i, j: (i, j)),
        out_shape=jax.ShapeDtypeStruct((m, d), BF16),
        compiler_params=_params(2),
        name="gated_merge",
    )(h, ya, yd, yx, w_gate, w_gate, w_gate, b_gate, b_gate, b_gate, w_a, w_d, w_x)


def _out_proj_kernel(z_ref, w_ref, x_ref, gpost_ref, gpre_ref, x1_ref, h2_ref):
    for r0 in range(0, z_ref.shape[0], EPILOGUE_ROWS):
        rows = slice(r0, r0 + EPILOGUE_ROWS)
        y = _dot(z_ref[rows, :], w_ref[...])
        x1 = x_ref[rows, :] + _rmsnorm_rows(y, gpost_ref[...])
        x1_ref[rows, :] = x1
        h2_ref[rows, :] = _rmsnorm_rows(x1, gpre_ref[...]).astype(h2_ref.dtype)


def out_proj_residual(z, w_o, x, g_post, g_next_pre):
    m, d = x.shape
    tm = _tile(m, 512)
    row = lambda: pl.BlockSpec((tm, d), lambda i: (i, 0))
    vec = lambda: pl.BlockSpec((1, d), lambda i: (0, 0))
    return pl.pallas_call(
        _out_proj_kernel,
        grid=(m // tm,),
        in_specs=[row(), pl.BlockSpec((d, d), lambda i: (0, 0)), row(), vec(), vec()],
        out_specs=[row(), row()],
        out_shape=[jax.ShapeDtypeStruct((m, d), F32), jax.ShapeDtypeStruct((m, d), F32)],
        compiler_params=_params(1),
        name="out_proj_residual",
    )(z, w_o, x, g_post.reshape(1, d), g_next_pre.reshape(1, d))


def _ffn_up_kernel(h_ref, wg_ref, wv_ref, cwg_ref, cwv_ref, o_ref, carry_g, carry_v, *, tiles_per_seq):
    i = pl.program_id(0)
    j = pl.program_id(1)
    start = i % tiles_per_seq == 0
    prev_g = _load_carry(carry_g, j, start)
    prev_v = _load_carry(carry_v, j, start)
    for r0 in range(0, h_ref.shape[0], FFN_UP_ROWS):
        h = h_ref[r0:r0 + FFN_UP_ROWS, :]
        zg = _dot(h, wg_ref[...])
        zv = _dot(h, wv_ref[...])
        gate = _causal_conv3(zg, prev_g, cwg_ref[...])
        val = _causal_conv3(zv, prev_v, cwv_ref[...])
        o_ref[r0:r0 + FFN_UP_ROWS, :] = (gate * jax.nn.sigmoid(gate) * val).astype(o_ref.dtype)
        prev_g = zg[FFN_UP_ROWS - SUBLANES:, :]
        prev_v = zv[FFN_UP_ROWS - SUBLANES:, :]
    carry_g[j] = prev_g
    carry_v[j] = prev_v


def ffn_up(h, w_up, conv_w, seq):
    m, d = h.shape
    d_ff = w_up.shape[1] // 2
    tm = _tile(seq, 1024)
    tn = 512
    nj = d_ff // tn
    kern = functools.partial(_ffn_up_kernel, tiles_per_seq=seq // tm)
    return pl.pallas_call(
        kern,
        grid=(m // tm, nj),
        in_specs=[pl.BlockSpec((tm, d), lambda i, j: (i, 0)),
                  pl.BlockSpec((d, tn), lambda i, j: (0, j)),
                  pl.BlockSpec((d, tn), lambda i, j: (0, nj + j)),
                  pl.BlockSpec((CONV_K, tn), lambda i, j: (0, j)),
                  pl.BlockSpec((CONV_K, tn), lambda i, j: (0, nj + j))],
        out_specs=pl.BlockSpec((tm, tn), lambda i, j: (i, j)),
        out_shape=jax.ShapeDtypeStruct((m, d_ff), BF16),
        scratch_shapes=[pltpu.VMEM((nj, SUBLANES, tn), F32), pltpu.VMEM((nj, SUBLANES, tn), F32)],
        compiler_params=_params(2),
        name="ffn_up",
    )(h, w_up, w_up, conv_w, conv_w)


def _ffn_down_kernel(a_ref, w_ref, x_ref, g_ref, o_ref, acc_ref):
    k = pl.program_id(1)
    last = pl.num_programs(1) - 1

    @pl.when(k == 0)
    def _():
        acc_ref[...] = _dot(a_ref[...], w_ref[...])

    @pl.when((k > 0) & (k < last))
    def _():
        acc_ref[...] += _dot(a_ref[...], w_ref[...])

    @pl.when(k == last)
    def _():
        for r0 in range(0, a_ref.shape[0], EPILOGUE_ROWS):
            rows = slice(r0, r0 + EPILOGUE_ROWS)
            y = acc_ref[rows, :] + _dot(a_ref[rows, :], w_ref[...])
            o_ref[rows, :] = x_ref[rows, :] + _rmsnorm_rows(y, g_ref[...])


def ffn_down(a, w_down, x, g_post):
    m, d = x.shape
    d_ff = a.shape[1]
    tm = _tile(m, 1024)
    tk = 512
    assert d_ff // tk >= 2
    row = lambda: pl.BlockSpec((tm, d), lambda i, k: (i, 0))
    return pl.pallas_call(
        _ffn_down_kernel,
        grid=(m // tm, d_ff // tk),
        in_specs=[pl.BlockSpec((tm, tk), lambda i, k: (i, k)),
                  pl.BlockSpec((tk, d), lambda i, k: (k, 0)),
                  row(), pl.BlockSpec((1, d), lambda i, k: (0, 0))],
        out_specs=row(),
        out_shape=jax.ShapeDtypeStruct((m, d), F32),
        scratch_shapes=[pltpu.VMEM((tm, d), F32)],
        compiler_params=_params(2),
        name="ffn_down",
    )(a, w_down, x, g_post.reshape(1, d))


def kernel(x, mem, g_mix_pre, w_in, w_gate, b_gate, conv_w, w_conv_out, lambda_q1, lambda_k1, lambda_q2, lambda_k2, g_diff_sub, w_diff_out, g_mem, w_mem_kv, w_cross_out, w_o, g_mix_post, g_ffn_pre, w_up, ffn_conv_w, w_down, g_ffn_post):
    bsz, seq, d = x.shape
    mem_len = mem.shape[1]
    depth = w_in.shape[0]
    m = bsz * seq
    q_lo = N_BRANCH * CONV_WIDTH
    qkvx_width = 2 * DIFF_QK_WIDTH + DIFF_WIDTH + CROSS_WIDTH
    qkvx_scale = jnp.where(jnp.arange(qkvx_width) < DIFF_QK_WIDTH,
                           LOG2E * DIFF_QK_DIM ** -0.5, 1.0).astype(F32)
    kv_scale = jnp.ones((2 * CROSS_WIDTH,), F32)

    xf = x.reshape(m, d)
    for l in range(depth):
        lambda_init = 0.8 - 0.6 * math.exp(-0.3 * l)
        w_in_b = w_in[l].astype(BF16)
        ya, h = conv_mixer(xf, g_mix_pre[l], w_in_b, conv_w[l], seq)
        qkvx = project(h, w_in_b, qkvx_scale, q_lo // 1024).reshape(bsz, seq, qkvx_width)

        lam_rows = jnp.stack([lambda_q1[l], lambda_k1[l], lambda_q2[l], lambda_k2[l]])
        yd = diff_attention(qkvx, lam_rows, g_diff_sub[l], bsz, seq, lambda_init)

        mem_n = rmsnorm_bf16(mem.reshape(bsz * mem_len, d), g_mem[l])
        kv = project(mem_n, w_mem_kv[l].astype(BF16), kv_scale, 0, tm_want=bsz * mem_len)
        yx = cross_attention(qkvx, kv.reshape(bsz, mem_len, 2 * CROSS_WIDTH), bsz, seq)

        merged = gated_merge(h, ya, yd.reshape(m, DIFF_WIDTH), yx.reshape(m, CROSS_WIDTH),
                             w_gate[l].astype(BF16), b_gate[l].reshape(1, -1),
                             w_conv_out[l].astype(BF16), w_diff_out[l].astype(BF16),
                             w_cross_out[l].astype(BF16))
        xf, h2 = out_proj_residual(merged, w_o[l].astype(BF16), xf, g_mix_post[l], g_ffn_pre[l])

        act = ffn_up(h2, w_up[l].astype(BF16), ffn_conv_w[l], seq)
        xf = ffn_down(act, w_down[l].astype(BF16), xf, g_ffn_post[l])
    return xf.reshape(bsz, seq, d)
```

```python
import functools
import math

import jax
import jax.numpy as jnp
from jax import lax
from jax.experimental import pallas as pl
from jax.experimental.pallas import tpu as pltpu

EPS = 1e-6
CHUNK = 64
CONV_WIDTH = 1024
CONV_K = 3
DIFF_HEADS = 8
DIFF_QK_DIM = 64
DIFF_V_DIM = 2 * DIFF_QK_DIM
DIFF_QK_WIDTH = DIFF_HEADS * 2 * DIFF_QK_DIM
DIFF_WIDTH = DIFF_HEADS * DIFF_V_DIM
CROSS_HEADS = 4
CROSS_HEAD_DIM = 256
CROSS_WIDTH = CROSS_HEADS * CROSS_HEAD_DIM
N_BRANCH = 3
LOG2E = math.log2(math.e)
MAX_EXCESS = 30.0
GROUP = 8

SUBLANES = 8
BF16_ROWS = 16
FFN_UP_ROWS = 128
EPILOGUE_ROWS = 256
VMEM_LIMIT = 56 * 1024 * 1024

F32 = jnp.float32
BF16 = jnp.bfloat16


def _params(n_axes):
    return pltpu.CompilerParams(
        dimension_semantics=("arbitrary",) * n_axes, vmem_limit_bytes=VMEM_LIMIT)


def _dot(a, b):
    return lax.dot_general(a, b, (((1,), (0,)), ((), ())), preferred_element_type=F32)


def _dot_nt(a, b):
    return lax.dot_general(a, b, (((1,), (1,)), ((), ())), preferred_element_type=F32)


def _tile(n, want):
    t = min(n, want)
    assert n % t == 0, (n, want)
    return t


def _rmsnorm_rows(x, g):
    ms = jnp.mean(x * x, axis=-1, keepdims=True)
    return x * lax.rsqrt(ms + EPS) * g


def _rmsnorm_kernel(x_ref, g_ref, o_ref):
    o_ref[...] = _rmsnorm_rows(x_ref[...], g_ref[...]).astype(o_ref.dtype)


def rmsnorm_bf16(x, g):
    m, d = x.shape
    tm = _tile(m, 512)
    return pl.pallas_call(
        _rmsnorm_kernel,
        grid=(m // tm,),
        in_specs=[pl.BlockSpec((tm, d), lambda i: (i, 0)),
                  pl.BlockSpec((1, d), lambda i: (0, 0))],
        out_specs=pl.BlockSpec((tm, d), lambda i: (i, 0)),
        out_shape=jax.ShapeDtypeStruct((m, d), BF16),
        compiler_params=_params(1),
        name="rmsnorm",
    )(x, g.reshape(1, d))


def _shift_rows(u, prev, s):
    rolled = pltpu.roll(u, s, axis=0)
    rprev = pltpu.roll(prev, s, axis=0)
    rows = lax.broadcasted_iota(jnp.int32, prev.shape, 0)
    first = jnp.where(rows < s, rprev, rolled[0:SUBLANES])
    return jnp.concatenate([first, rolled[SUBLANES:]], axis=0)


def _causal_conv3(u, prev, w):
    return (w[2:3] * u + w[1:2] * _shift_rows(u, prev, 1) + w[0:1] * _shift_rows(u, prev, 2))


def _load_carry(carry_ref, j, at_seq_start):
    @pl.when(at_seq_start)
    def _():
        carry_ref[j] = jnp.zeros(carry_ref.shape[1:], carry_ref.dtype)
    return carry_ref[j]


def _conv_mixer_kernel(x_ref, g_ref, wb_ref, wc_ref, wv_ref, cw_ref, o_ref, h_ref, carry_ref, *, tiles_per_seq):
    i = pl.program_id(0)
    j = pl.program_id(1)

    @pl.when(j == 0)
    def _():
        h_ref[...] = _rmsnorm_rows(x_ref[...], g_ref[...]).astype(h_ref.dtype)

    h = h_ref[...]
    u = _dot(h, wc_ref[...]) * _dot(h, wv_ref[...])
    prev = _load_carry(carry_ref, j, i % tiles_per_seq == 0)
    conv = _causal_conv3(u, prev, cw_ref[...])
    carry_ref[j] = u[u.shape[0] - SUBLANES:, :]
    o_ref[...] = (_dot(h, wb_ref[...]) * conv).astype(o_ref.dtype)


def conv_mixer(x, g_pre, w_in, conv_w, seq):
    m, d = x.shape
    tm = _tile(seq, 1024)
    tn = 512
    nj = CONV_WIDTH // tn
    kern = functools.partial(_conv_mixer_kernel, tiles_per_seq=seq // tm)
    return pl.pallas_call(
        kern,
        grid=(m // tm, nj),
        in_specs=[pl.BlockSpec((tm, d), lambda i, j: (i, 0)),
                  pl.BlockSpec((1, d), lambda i, j: (0, 0)),
                  pl.BlockSpec((d, tn), lambda i, j: (0, j)),
                  pl.BlockSpec((d, tn), lambda i, j: (0, nj + j)),
                  pl.BlockSpec((d, tn), lambda i, j: (0, 2 * nj + j)),
                  pl.BlockSpec((CONV_K, tn), lambda i, j: (0, j))],
        out_specs=[pl.BlockSpec((tm, tn), lambda i, j: (i, j)),
                   pl.BlockSpec((tm, d), lambda i, j: (i, 0))],
        out_shape=[jax.ShapeDtypeStruct((m, CONV_WIDTH), BF16), jax.ShapeDtypeStruct((m, d), BF16)],
        scratch_shapes=[pltpu.VMEM((nj, SUBLANES, tn), F32)],
        compiler_params=_params(2),
        name="conv_mixer",
    )(x, g_pre.reshape(1, d), w_in, w_in, w_in, conv_w)


def _proj_kernel(h_ref, w_ref, scale_ref, o_ref):
    o_ref[...] = (_dot(h_ref[...], w_ref[...]) * scale_ref[...]).astype(o_ref.dtype)


def project(h, w, col_scale, col_block0, tm_want=1024, tn=1024):
    m, d = h.shape
    n_out = col_scale.shape[0]
    tm = _tile(m, tm_want)
    return pl.pallas_call(
        _proj_kernel,
        grid=(m // tm, n_out // tn),
        in_specs=[pl.BlockSpec((tm, d), lambda i, j: (i, 0)),
                  pl.BlockSpec((d, tn), lambda i, j: (0, col_block0 + j)),
                  pl.BlockSpec((1, tn), lambda i, j: (0, j))],
        out_specs=pl.BlockSpec((tm, tn), lambda i, j: (i, j)),
        out_shape=jax.ShapeDtypeStruct((m, n_out), BF16),
        compiler_params=_params(2),
        name="project",
    )(h, w, col_scale.reshape(1, n_out))


def _col_max(s):
    r = s.shape[0]
    while r > SUBLANES:
        r //= 2
        s = jnp.maximum(s[:r], s[r:])
    return jnp.max(s, axis=0, keepdims=True)


def _diff_attn_kernel(lam_ref, q_ref, k_ref, v_ref, gsub_ref, o_ref,
                      vt_ref, s_ref, p_ref, alpha_ref, m_ref, excess_ref, acc_ref, *, tq, tk, lambda_init):
    qi = pl.program_id(2)
    seq = k_ref.shape[1]
    dv = DIFF_V_DIM
    assert tq == 2 * tk

    @pl.when(qi == 0)
    def _():
        vt_ref[dv:, :] = jnp.ones((BF16_ROWS, seq), BF16)

        def body(c, carry):
            r = pl.multiple_of(c * tq, tq)
            vt_ref[0:dv, pl.ds(r, tq)] = v_ref[0, pl.ds(r, tq), :].astype(F32).T.astype(BF16)
            return carry
        lax.fori_loop(0, seq // tq, body, 0)

    q = q_ref[0].astype(F32)
    lane = lax.broadcasted_iota(jnp.int32, q.shape, 1)
    qz = (jnp.where(lane < DIFF_QK_DIM, q, 0.0).astype(BF16),
          jnp.where(lane >= DIFF_QK_DIM, q, 0.0).astype(BF16))

    def k_tile(t):
        return k_ref[0, pl.ds(pl.multiple_of(t * tk, tk), tk), :]

    def vt_tile(t):
        return vt_ref[:, pl.ds(pl.multiple_of(t * tk, tk), tk)]

    key_chunk = lax.broadcasted_iota(jnp.int32, (tk, tq), 0) // CHUNK
    qry_chunk = lax.broadcasted_iota(jnp.int32, (tk, tq), 1) // CHUNK
    diag_masks = (key_chunk <= qry_chunk, key_chunk + tk // CHUNK <= qry_chunk)

    def exact_tile(t):
        k_t = k_tile(t)
        vt_t = vt_tile(t)
        for c in range(2):
            s = _dot_nt(k_t, qz[c])
            m_old = m_ref[c]
            m_new = jnp.maximum(m_old, _col_max(s))
            p = jnp.exp2(s - m_new).astype(BF16)
            acc_ref[c] = jnp.exp2(m_old - m_new) * acc_ref[c] + _dot(vt_t, p)
            m_ref[c] = m_new

    def exact_diagonal():
        k_d = [k_tile(2 * qi + d) for d in range(2)]
        vt_d = [vt_tile(2 * qi + d) for d in range(2)]
        for c in range(2):
            s = [jnp.where(diag_masks[d], _dot_nt(k_d[d], qz[c]), -jnp.inf) for d in range(2)]
            m = jnp.maximum(_col_max(s[0]), _col_max(s[1]))
            acc_ref[c] = (_dot(vt_d[0], jnp.exp2(s[0] - m).astype(BF16))
                          + _dot(vt_d[1], jnp.exp2(s[1] - m).astype(BF16)))
            m_ref[c] = m

    def fast_group(t0, n, masks=None):
        assert 2 <= n <= GROUP
        slots = range(GROUP - n, GROUP)
        m_frozen = [m_ref[c] for c in range(2)]
        tile_max = [None, None]
        if masks is None:
            k_next = k_tile(t0 + n)
            s_next = [_dot_nt(k_next, qz[c]) for c in range(2)]
        for c in range(2):
            s = s_ref[c]
            if masks is not None:
                s = jnp.where(masks[0], s, -jnp.inf)
            tile_max[c] = _col_max(s)
            p_ref[slots[0], c] = jnp.exp2(s - m_frozen[c]).astype(BF16)
            if masks is None:
                s_ref[c] = s_next[c]
        vt_pending = vt_tile(jnp.maximum(t0 - 1, 0))
        acc = [alpha_ref[c] * (acc_ref[c] + _dot(vt_pending, p_ref[GROUP - 1, c])) for c in range(2)]
        for i in range(1, n):
            k_t = k_tile(t0 + i)
            for c in range(2):
                s = _dot_nt(k_t, qz[c])
                if masks is not None:
                    s = jnp.where(masks[i], s, -jnp.inf)
                tile_max[c] = jnp.maximum(tile_max[c], _col_max(s))
                p_ref[slots[i], c] = jnp.exp2(s - m_frozen[c]).astype(BF16)
            vt_t = vt_tile(t0 + i - 1)
            for c in range(2):
                acc[c] = acc[c] + _dot(vt_t, p_ref[slots[i - 1], c])
        if masks is not None:
            vt_t = vt_tile(t0 + n - 1)
            for c in range(2):
                acc[c] = acc[c] + _dot(vt_t, p_ref[slots[n - 1], c])
        for c in range(2):
            acc_ref[c] = acc[c]
            m_new = jnp.maximum(m_frozen[c], tile_max[c])
            alpha_ref[c] = jnp.exp2(m_frozen[c] - m_new)
            excess_ref[c] = jnp.maximum(excess_ref[c], tile_max[c] - m_frozen[c])
            m_ref[c] = m_new

    for c in range(2):
        s0 = _dot_nt(k_tile(0), qz[c])
        s_ref[c] = s0
        m_ref[c] = _col_max(s0[0:CHUNK])
    acc_ref[...] = jnp.zeros(acc_ref.shape, F32)
    p_ref[GROUP - 1] = jnp.zeros(p_ref.shape[1:], BF16)
    alpha_ref[...] = jnp.ones(alpha_ref.shape, F32)
    excess_ref[...] = jnp.zeros(excess_ref.shape, F32)
    n_tiles = 2 * qi
    rem = n_tiles % GROUP

    @pl.when(rem % 4 > 0)
    def _():
        fast_group(0, 2)

    @pl.when(rem >= 4)
    def _():
        fast_group(rem % 4, 4)

    def group(j, carry):
        fast_group(rem + GROUP * j, GROUP)
        return carry
    lax.fori_loop(0, n_tiles // GROUP, group, 0)
    fast_group(n_tiles, 2, diag_masks)

    worst = jnp.max(jnp.maximum(excess_ref[0], excess_ref[1]), axis=-1, keepdims=True)

    @pl.when(worst[0, 0] > MAX_EXCESS)
    def _():
        exact_diagonal()

        def body(t, carry):
            exact_tile(t)
            return carry
        lax.fori_loop(0, 2 * qi, body, 0)

    lv = lam_ref[...]
    lam = (jnp.exp(jnp.sum(lv[0:1] * lv[1:2], axis=-1, keepdims=True))
           - jnp.exp(jnp.sum(lv[2:3] * lv[3:4], axis=-1, keepdims=True)) + lambda_init)
    a1 = acc_ref[0]
    a2 = acc_ref[1]
    o = a1[0:dv] / a1[dv:dv + 1] - lam * (a2[0:dv] / a2[dv:dv + 1])
    ms = jnp.mean(o * o, axis=0, keepdims=True)
    y = o * lax.rsqrt(ms + EPS) * gsub_ref[...]
    y = y * (1.0 - lambda_init)
    o_ref[0] = y.T.astype(o_ref.dtype)


def diff_attention(qkvx, lam_rows, g_sub, bsz, seq, lambda_init):
    tq = _tile(seq, 512)
    tk = tq // 2
    nq = seq // tq
    dv = DIFF_V_DIM
    kern = functools.partial(_diff_attn_kernel, tq=tq, tk=tk, lambda_init=lambda_init)
    return pl.pallas_call(
        kern,
        grid=(bsz, DIFF_HEADS, nq),
        in_specs=[pl.BlockSpec((4, DIFF_QK_DIM), lambda b, h, i: (0, 0)),
                  pl.BlockSpec((1, tq, dv), lambda b, h, i: (b, i, h)),
                  pl.BlockSpec((1, seq, dv), lambda b, h, i: (b, 0, DIFF_HEADS + h)),
                  pl.BlockSpec((1, seq, dv), lambda b, h, i: (b, 0, 2 * DIFF_HEADS + h)),
                  pl.BlockSpec((dv, 1), lambda b, h, i: (0, 0))],
        out_specs=pl.BlockSpec((1, tq, dv), lambda b, h, i: (b, i, h)),
        out_shape=jax.ShapeDtypeStruct((bsz, seq, DIFF_WIDTH), BF16),
        scratch_shapes=[pltpu.VMEM((dv + BF16_ROWS, seq), BF16),
                        pltpu.VMEM((2, tk, tq), F32),
                        pltpu.VMEM((GROUP, 2, tk, tq), BF16),
                        pltpu.VMEM((2, 1, tq), F32),
                        pltpu.VMEM((2, 1, tq), F32),
                        pltpu.VMEM((2, 1, tq), F32),
                        pltpu.VMEM((2, dv + BF16_ROWS, tq), F32)],
        compiler_params=_params(3),
        name="diff_attention",
    )(lam_rows, qkvx, qkvx, qkvx, g_sub.reshape(dv, 1))


def _cross_attn_kernel(xq_ref, kv_ref, o_ref):
    hd = CROSS_HEAD_DIM
    for h in range(CROSS_HEADS):
        q = xq_ref[0, :, h * hd:(h + 1) * hd]
        k = kv_ref[0, :, h * hd:(h + 1) * hd]
        v = kv_ref[0, :, CROSS_WIDTH + h * hd:CROSS_WIDTH + (h + 1) * hd]
        s = _dot_nt(q, k) * (hd ** -0.5)
        p = jnp.exp(s - jnp.max(s, axis=-1, keepdims=True))
        l = jnp.sum(p, axis=-1, keepdims=True)
        o_ref[0, :, h * hd:(h + 1) * hd] = (_dot(p.astype(BF16), v) / l).astype(o_ref.dtype)


def cross_attention(qkvx, kv, bsz, seq):
    tm = _tile(seq, 512)
    mem_len = kv.shape[1]
    return pl.pallas_call(
        _cross_attn_kernel,
        grid=(bsz, seq // tm),
        in_specs=[pl.BlockSpec((1, tm, CROSS_WIDTH), lambda b, i: (b, i, 3)),
                  pl.BlockSpec((1, mem_len, 2 * CROSS_WIDTH), lambda b, i: (b, 0, 0))],
        out_specs=pl.BlockSpec((1, tm, CROSS_WIDTH), lambda b, i: (b, i, 0)),
        out_shape=jax.ShapeDtypeStruct((bsz, seq, CROSS_WIDTH), BF16),
        compiler_params=_params(2),
        name="cross_attention",
    )(qkvx, kv)


def _merge_kernel(h_ref, ya_ref, yd_ref, yx_ref, wga_ref, wgb_ref, wgc_ref,
                  ba_ref, bb_ref, bc_ref, wa_ref, wd_ref, wx_ref, o_ref):
    h = h_ref[...]
    merged = jax.nn.sigmoid(_dot(h, wga_ref[...]) + ba_ref[...]) * _dot(ya_ref[...], wa_ref[...])
    merged += jax.nn.sigmoid(_dot(h, wgb_ref[...]) + bb_ref[...]) * _dot(yd_ref[...], wd_ref[...])
    merged += jax.nn.sigmoid(_dot(h, wgc_ref[...]) + bc_ref[...]) * _dot(yx_ref[...], wx_ref[...])
    o_ref[...] = merged.astype(o_ref.dtype)


def gated_merge(h, ya, yd, yx, w_gate, b_gate, w_a, w_d, w_x):
    m, d = h.shape
    tm = _tile(m, 1024)
    tn = 512
    nj = d // tn
    act = lambda width: pl.BlockSpec((tm, width), lambda i, j: (i, 0))
    gate_w = lambda br: pl.BlockSpec((d, tn), lambda i, j: (0, br * nj + j))
    gate_b = lambda br: pl.BlockSpec((1, tn), lambda i, j: (0, br * nj + j))
    out_w = lambda width: pl.BlockSpec((width, tn), lambda i, j: (0, j))
    return pl.pallas_call(
        _merge_kernel,
        grid=(m // tm, nj),
        in_specs=[act(d), act(CONV_WIDTH), act(DIFF_WIDTH), act(CROSS_WIDTH),
                  gate_w(0), gate_w(1), gate_w(2), gate_b(0), gate_b(1), gate_b(2),
                  out_w(CONV_WIDTH), out_w(DIFF_WIDTH), out_w(CROSS_WIDTH)],
        out_specs=pl.BlockSpec((tm, tn), lambda i, j: (i, j)),
        out_shape=jax.ShapeDtypeStruct((m, d), BF16),
        compiler_params=_params(2),
        name="gated_merge",
    )(h, ya, yd, yx, w_gate, w_gate, w_gate, b_gate, b_gate, b_gate, w_a, w_d, w_x)


def _out_proj_kernel(z_ref, w_ref, x_ref, gpost_ref, gpre_ref, x1_ref, h2_ref):
    for r0 in range(0, z_ref.shape[0], EPILOGUE_ROWS):
        rows = slice(r0, r0 + EPILOGUE_ROWS)
        y = _dot(z_ref[rows, :], w_ref[...])
        x1 = x_ref[rows, :] + _rmsnorm_rows(y, gpost_ref[...])
        x1_ref[rows, :] = x1
        h2_ref[rows, :] = _rmsnorm_rows(x1, gpre_ref[...]).astype(h2_ref.dtype)


def out_proj_residual(z, w_o, x, g_post, g_next_pre):
    m, d = x.shape
    tm = _tile(m, 512)
    row = lambda: pl.BlockSpec((tm, d), lambda i: (i, 0))
    vec = lambda: pl.BlockSpec((1, d), lambda i: (0, 0))
    return pl.pallas_call(
        _out_proj_kernel,
        grid=(m // tm,),
        in_specs=[row(), pl.BlockSpec((d, d), lambda i: (0, 0)), row(), vec(), vec()],
        out_specs=[row(), row()],
        out_shape=[jax.ShapeDtypeStruct((m, d), F32), jax.ShapeDtypeStruct((m, d), F32)],
        compiler_params=_params(1),
        name="out_proj_residual",
    )(z, w_o, x, g_post.reshape(1, d), g_next_pre.reshape(1, d))


def _ffn_up_kernel(h_ref, wg_ref, wv_ref, cwg_ref, cwv_ref, o_ref, carry_g, carry_v, *, tiles_per_seq):
    i = pl.program_id(0)
    j = pl.program_id(1)
    start = i % tiles_per_seq == 0
    prev_g = _load_carry(carry_g, j, start)
    prev_v = _load_carry(carry_v, j, start)
    for r0 in range(0, h_ref.shape[0], FFN_UP_ROWS):
        h = h_ref[r0:r0 + FFN_UP_ROWS, :]
        zg = _dot(h, wg_ref[...])
        zv = _dot(h, wv_ref[...])
        gate = _causal_conv3(zg, prev_g, cwg_ref[...])
        val = _causal_conv3(zv, prev_v, cwv_ref[...])
        o_ref[r0:r0 + FFN_UP_ROWS, :] = (gate * jax.nn.sigmoid(gate) * val).astype(o_ref.dtype)
        prev_g = zg[FFN_UP_ROWS - SUBLANES:, :]
        prev_v = zv[FFN_UP_ROWS - SUBLANES:, :]
    carry_g[j] = prev_g
    carry_v[j] = prev_v


def ffn_up(h, w_up, conv_w, seq):
    m, d = h.shape
    d_ff = w_up.shape[1] // 2
    tm = _tile(seq, 1024)
    tn = 512
    nj = d_ff // tn
    kern = functools.partial(_ffn_up_kernel, tiles_per_seq=seq // tm)
    return pl.pallas_call(
        kern,
        grid=(m // tm, nj),
        in_specs=[pl.BlockSpec((tm, d), lambda i, j: (i, 0)),
                  pl.BlockSpec((d, tn), lambda i, j: (0, j)),
                  pl.BlockSpec((d, tn), lambda i, j: (0, nj + j)),
                  pl.BlockSpec((CONV_K, tn), lambda i, j: (0, j)),
                  pl.BlockSpec((CONV_K, tn), lambda i, j: (0, nj + j))],
        out_specs=pl.BlockSpec((tm, tn), lambda i, j: (i, j)),
        out_shape=jax.ShapeDtypeStruct((m, d_ff), BF16),
        scratch_shapes=[pltpu.VMEM((nj, SUBLANES, tn), F32), pltpu.VMEM((nj, SUBLANES, tn), F32)],
        compiler_params=_params(2),
        name="ffn_up",
    )(h, w_up, w_up, conv_w, conv_w)


def _ffn_down_kernel(a_ref, w_ref, x_ref, g_ref, o_ref):
    for r0 in range(0, a_ref.shape[0], EPILOGUE_ROWS):
        rows = slice(r0, r0 + EPILOGUE_ROWS)
        y = _dot(a_ref[rows, :], w_ref[...])
        o_ref[rows, :] = x_ref[rows, :] + _rmsnorm_rows(y, g_ref[...])


def ffn_down(a, w_down, x, g_post):
    m, d = x.shape
    d_ff = a.shape[1]
    tm = _tile(m, 512)
    row = lambda: pl.BlockSpec((tm, d), lambda i: (i, 0))
    return pl.pallas_call(
        _ffn_down_kernel,
        grid=(m // tm,),
        in_specs=[pl.BlockSpec((tm, d_ff), lambda i: (i, 0)),
                  pl.BlockSpec((d_ff, d), lambda i: (0, 0)),
                  row(), pl.BlockSpec((1, d), lambda i: (0, 0))],
        out_specs=row(),
        out_shape=jax.ShapeDtypeStruct((m, d), F32),
        compiler_params=_params(1),
        name="ffn_down",
    )(a, w_down, x, g_post.reshape(1, d))


def kernel(x, mem, g_mix_pre, w_in, w_gate, b_gate, conv_w, w_conv_out, lambda_q1, lambda_k1, lambda_q2, lambda_k2, g_diff_sub, w_diff_out, g_mem, w_mem_kv, w_cross_out, w_o, g_mix_post, g_ffn_pre, w_up, ffn_conv_w, w_down, g_ffn_post):
    bsz, seq, d = x.shape
    mem_len = mem.shape[1]
    depth = w_in.shape[0]
    m = bsz * seq
    q_lo = N_BRANCH * CONV_WIDTH
    qkvx_width = 2 * DIFF_QK_WIDTH + DIFF_WIDTH + CROSS_WIDTH
    qkvx_scale = jnp.where(jnp.arange(qkvx_width) < DIFF_QK_WIDTH,
                           LOG2E * DIFF_QK_DIM ** -0.5, 1.0).astype(F32)
    kv_scale = jnp.ones((2 * CROSS_WIDTH,), F32)

    xf = x.reshape(m, d)
    for l in range(depth):
        lambda_init = 0.8 - 0.6 * math.exp(-0.3 * l)
        w_in_b = w_in[l].astype(BF16)
        ya, h = conv_mixer(xf, g_mix_pre[l], w_in_b, conv_w[l], seq)
        qkvx = project(h, w_in_b, qkvx_scale, q_lo // 1024).reshape(bsz, seq, qkvx_width)

        lam_rows = jnp.stack([lambda_q1[l], lambda_k1[l], lambda_q2[l], lambda_k2[l]])
        yd = diff_attention(qkvx, lam_rows, g_diff_sub[l], bsz, seq, lambda_init)

        mem_n = rmsnorm_bf16(mem.reshape(bsz * mem_len, d), g_mem[l])
        kv = project(mem_n, w_mem_kv[l].astype(BF16), kv_scale, 0, tm_want=bsz * mem_len)
        yx = cross_attention(qkvx, kv.reshape(bsz, mem_len, 2 * CROSS_WIDTH), bsz, seq)

        merged = gated_merge(h, ya, yd.reshape(m, DIFF_WIDTH), yx.reshape(m, CROSS_WIDTH),
                             w_gate[l].astype(BF16), b_gate[l].reshape(1, -1),
                             w_conv_out[l].astype(BF16), w_diff_out[l].astype(BF16),
                             w_cross_out[l].astype(BF16))
        xf, h2 = out_proj_residual(merged, w_o[l].astype(BF16), xf, g_mix_post[l], g_ffn_pre[l])

        act = ffn_up(h2, w_up[l].astype(BF16), ffn_conv_w[l], seq)
        xf = ffn_down(act, w_down[l].astype(BF16), xf, g_ffn_post[l])
    return xf.reshape(bsz, seq, d)
```

```python
import functools
import math

import jax
import jax.numpy as jnp
from jax import lax
from jax.experimental import pallas as pl
from jax.experimental.pallas import tpu as pltpu

EPS = 1e-6
CHUNK = 64
CONV_WIDTH = 1024
CONV_K = 3
DIFF_HEADS = 8
DIFF_QK_DIM = 64
DIFF_V_DIM = 2 * DIFF_QK_DIM
DIFF_QK_WIDTH = DIFF_HEADS * 2 * DIFF_QK_DIM
DIFF_WIDTH = DIFF_HEADS * DIFF_V_DIM
CROSS_HEADS = 4
CROSS_HEAD_DIM = 256
CROSS_WIDTH = CROSS_HEADS * CROSS_HEAD_DIM
N_BRANCH = 3
LOG2E = math.log2(math.e)
MAX_EXCESS = 30.0
GROUP = 8

SUBLANES = 8
BF16_ROWS = 16
FFN_UP_ROWS = 128
EPILOGUE_ROWS = 256
VMEM_LIMIT = 56 * 1024 * 1024

F32 = jnp.float32
BF16 = jnp.bfloat16


def _params(n_axes):
    return pltpu.CompilerParams(
        dimension_semantics=("arbitrary",) * n_axes, vmem_limit_bytes=VMEM_LIMIT)


def _dot(a, b):
    return lax.dot_general(a, b, (((1,), (0,)), ((), ())), preferred_element_type=F32)


def _dot_nt(a, b):
    return lax.dot_general(a, b, (((1,), (1,)), ((), ())), preferred_element_type=F32)


def _tile(n, want):
    t = min(n, want)
    assert n % t == 0, (n, want)
    return t


def _rmsnorm_rows(x, g):
    ms = jnp.mean(x * x, axis=-1, keepdims=True)
    return x * lax.rsqrt(ms + EPS) * g


def _rmsnorm_kernel(x_ref, g_ref, o_ref):
    o_ref[...] = _rmsnorm_rows(x_ref[...], g_ref[...]).astype(o_ref.dtype)


def rmsnorm_bf16(x, g):
    m, d = x.shape
    tm = _tile(m, 512)
    return pl.pallas_call(
        _rmsnorm_kernel,
        grid=(m // tm,),
        in_specs=[pl.BlockSpec((tm, d), lambda i: (i, 0)),
                  pl.BlockSpec((1, d), lambda i: (0, 0))],
        out_specs=pl.BlockSpec((tm, d), lambda i: (i, 0)),
        out_shape=jax.ShapeDtypeStruct((m, d), BF16),
        compiler_params=_params(1),
        name="rmsnorm",
    )(x, g.reshape(1, d))


def _shift_rows(u, prev, s):
    rolled = pltpu.roll(u, s, axis=0)
    rprev = pltpu.roll(prev, s, axis=0)
    rows = lax.broadcasted_iota(jnp.int32, prev.shape, 0)
    first = jnp.where(rows < s, rprev, rolled[0:SUBLANES])
    return jnp.concatenate([first, rolled[SUBLANES:]], axis=0)


def _causal_conv3(u, prev, w):
    return (w[2:3] * u + w[1:2] * _shift_rows(u, prev, 1) + w[0:1] * _shift_rows(u, prev, 2))


def _load_carry(carry_ref, j, at_seq_start):
    @pl.when(at_seq_start)
    def _():
        carry_ref[j] = jnp.zeros(carry_ref.shape[1:], carry_ref.dtype)
    return carry_ref[j]


def _conv_mixer_kernel(x_ref, g_ref, wb_ref, wc_ref, wv_ref, cw_ref, o_ref, h_ref, carry_ref, *, tiles_per_seq):
    i = pl.program_id(0)
    j = pl.program_id(1)

    @pl.when(j == 0)
    def _():
        h_ref[...] = _rmsnorm_rows(x_ref[...], g_ref[...]).astype(h_ref.dtype)

    h = h_ref[...]
    u = _dot(h, wc_ref[...]) * _dot(h, wv_ref[...])
    prev = _load_carry(carry_ref, j, i % tiles_per_seq == 0)
    conv = _causal_conv3(u, prev, cw_ref[...])
    carry_ref[j] = u[u.shape[0] - SUBLANES:, :]
    o_ref[...] = (_dot(h, wb_ref[...]) * conv).astype(o_ref.dtype)


def conv_mixer(x, g_pre, w_in, conv_w, seq):
    m, d = x.shape
    tm = _tile(seq, 1024)
    tn = 512
    nj = CONV_WIDTH // tn
    kern = functools.partial(_conv_mixer_kernel, tiles_per_seq=seq // tm)
    return pl.pallas_call(
        kern,
        grid=(m // tm, nj),
        in_specs=[pl.BlockSpec((tm, d), lambda i, j: (i, 0)),
                  pl.BlockSpec((1, d), lambda i, j: (0, 0)),
                  pl.BlockSpec((d, tn), lambda i, j: (0, j)),
                  pl.BlockSpec((d, tn), lambda i, j: (0, nj + j)),
                  pl.BlockSpec((d, tn), lambda i, j: (0, 2 * nj + j)),
                  pl.BlockSpec((CONV_K, tn), lambda i, j: (0, j))],
        out_specs=[pl.BlockSpec((tm, tn), lambda i, j: (i, j)),
                   pl.BlockSpec((tm, d), lambda i, j: (i, 0))],
        out_shape=[jax.ShapeDtypeStruct((m, CONV_WIDTH), BF16), jax.ShapeDtypeStruct((m, d), BF16)],
        scratch_shapes=[pltpu.VMEM((nj, SUBLANES, tn), F32)],
        compiler_params=_params(2),
        name="conv_mixer",
    )(x, g_pre.reshape(1, d), w_in, w_in, w_in, conv_w)


def _proj_kernel(h_ref, w_ref, scale_ref, o_ref):
    o_ref[...] = (_dot(h_ref[...], w_ref[...]) * scale_ref[...]).astype(o_ref.dtype)


def project(h, w, col_scale, col_block0, tm_want=1024, tn=1024):
    m, d = h.shape
    n_out = col_scale.shape[0]
    tm = _tile(m, tm_want)
    return pl.pallas_call(
        _proj_kernel,
        grid=(m // tm, n_out // tn),
        in_specs=[pl.BlockSpec((tm, d), lambda i, j: (i, 0)),
                  pl.BlockSpec((d, tn), lambda i, j: (0, col_block0 + j)),
                  pl.BlockSpec((1, tn), lambda i, j: (0, j))],
        out_specs=pl.BlockSpec((tm, tn), lambda i, j: (i, j)),
        out_shape=jax.ShapeDtypeStruct((m, n_out), BF16),
        compiler_params=_params(2),
        name="project",
    )(h, w, col_scale.reshape(1, n_out))


def _col_max(s):
    r = s.shape[0]
    while r > SUBLANES:
        r //= 2
        s = jnp.maximum(s[:r], s[r:])
    return jnp.max(s, axis=0, keepdims=True)


def _diff_attn_kernel(lam_ref, q_ref, k_ref, v_ref, gsub_ref, o_ref,
                      vt_ref, s_ref, p_ref, alpha_ref, m_ref, excess_ref, acc_ref, *, tq, tk, lambda_init):
    qi = pl.program_id(2)
    seq = k_ref.shape[1]
    dv = DIFF_V_DIM
    assert tq == 2 * tk

    @pl.when(qi == 0)
    def _():
        vt_ref[dv:, :] = jnp.ones((BF16_ROWS, seq), BF16)

        def body(c, carry):
            r = pl.multiple_of(c * tq, tq)
            vt_ref[0:dv, pl.ds(r, tq)] = v_ref[0, pl.ds(r, tq), :].astype(F32).T.astype(BF16)
            return carry
        lax.fori_loop(0, seq // tq, body, 0)

    qt = q_ref[0].astype(F32).T
    row = lax.broadcasted_iota(jnp.int32, qt.shape, 0)
    qz = (jnp.where(row < DIFF_QK_DIM, qt, 0.0).astype(BF16),
          jnp.where(row >= DIFF_QK_DIM, qt, 0.0).astype(BF16))

    def k_tile(t):
        return k_ref[0, pl.ds(pl.multiple_of(t * tk, tk), tk), :]

    def vt_tile(t):
        return vt_ref[:, pl.ds(pl.multiple_of(t * tk, tk), tk)]

    key_chunk = lax.broadcasted_iota(jnp.int32, (tk, tq), 0) // CHUNK
    qry_chunk = lax.broadcasted_iota(jnp.int32, (tk, tq), 1) // CHUNK
    diag_masks = (key_chunk <= qry_chunk, key_chunk + tk // CHUNK <= qry_chunk)

    def exact_tile(t):
        k_t = k_tile(t)
        vt_t = vt_tile(t)
        for c in range(2):
            s = _dot(k_t, qz[c])
            m_old = m_ref[c]
            m_new = jnp.maximum(m_old, _col_max(s))
            p = jnp.exp2(s - m_new).astype(BF16)
            acc_ref[c] = jnp.exp2(m_old - m_new) * acc_ref[c] + _dot(vt_t, p)
            m_ref[c] = m_new

    def exact_diagonal():
        k_d = [k_tile(2 * qi + d) for d in range(2)]
        vt_d = [vt_tile(2 * qi + d) for d in range(2)]
        for c in range(2):
            s = [jnp.where(diag_masks[d], _dot(k_d[d], qz[c]), -jnp.inf) for d in range(2)]
            m = jnp.maximum(_col_max(s[0]), _col_max(s[1]))
            acc_ref[c] = (_dot(vt_d[0], jnp.exp2(s[0] - m).astype(BF16))
                          + _dot(vt_d[1], jnp.exp2(s[1] - m).astype(BF16)))
            m_ref[c] = m

    def fast_group(t0, n, masks=None):
        assert 2 <= n <= GROUP
        slots = range(GROUP - n, GROUP)
        m_frozen = [m_ref[c] for c in range(2)]
        tile_max = [None, None]
        if masks is None:
            k_next = k_tile(t0 + n)
            s_next = [_dot(k_next, qz[c]) for c in range(2)]
        for c in range(2):
            s = s_ref[c]
            if masks is not None:
                s = jnp.where(masks[0], s, -jnp.inf)
            tile_max[c] = _col_max(s)
            p_ref[slots[0], c] = jnp.exp2(s - m_frozen[c]).astype(BF16)
            if masks is None:
                s_ref[c] = s_next[c]
        vt_pending = vt_tile(jnp.maximum(t0 - 1, 0))
        acc = [alpha_ref[c] * (acc_ref[c] + _dot(vt_pending, p_ref[GROUP - 1, c])) for c in range(2)]
        for i in range(1, n):
            k_t = k_tile(t0 + i)
            for c in range(2):
                s = _dot(k_t, qz[c])
                if masks is not None:
                    s = jnp.where(masks[i], s, -jnp.inf)
                tile_max[c] = jnp.maximum(tile_max[c], _col_max(s))
                p_ref[slots[i], c] = jnp.exp2(s - m_frozen[c]).astype(BF16)
            vt_t = vt_tile(t0 + i - 1)
            for c in range(2):
                acc[c] = acc[c] + _dot(vt_t, p_ref[slots[i - 1], c])
        if masks is not None:
            vt_t = vt_tile(t0 + n - 1)
            for c in range(2):
                acc[c] = acc[c] + _dot(vt_t, p_ref[slots[n - 1], c])
        for c in range(2):
            acc_ref[c] = acc[c]
            m_new = jnp.maximum(m_frozen[c], tile_max[c])
            alpha_ref[c] = jnp.exp2(m_frozen[c] - m_new)
            excess_ref[c] = jnp.maximum(excess_ref[c], tile_max[c] - m_frozen[c])
            m_ref[c] = m_new

    for c in range(2):
        s0 = _dot(k_tile(0), qz[c])
        s_ref[c] = s0
        m_ref[c] = _col_max(s0[0:CHUNK])
    acc_ref[...] = jnp.zeros(acc_ref.shape, F32)
    p_ref[GROUP - 1] = jnp.zeros(p_ref.shape[1:], BF16)
    alpha_ref[...] = jnp.ones(alpha_ref.shape, F32)
    excess_ref[...] = jnp.zeros(excess_ref.shape, F32)
    n_tiles = 2 * qi
    rem = n_tiles % GROUP

    @pl.when(rem % 4 > 0)
    def _():
        fast_group(0, 2)

    @pl.when(rem >= 4)
    def _():
        fast_group(rem % 4, 4)

    def group(j, carry):
        fast_group(rem + GROUP * j, GROUP)
        return carry
    lax.fori_loop(0, n_tiles // GROUP, group, 0)
    fast_group(n_tiles, 2, diag_masks)

    worst = jnp.max(jnp.maximum(excess_ref[0], excess_ref[1]), axis=-1, keepdims=True)

    @pl.when(worst[0, 0] > MAX_EXCESS)
    def _():
        exact_diagonal()

        def body(t, carry):
            exact_tile(t)
            return carry
        lax.fori_loop(0, 2 * qi, body, 0)

    lv = lam_ref[...]
    lam = (jnp.exp(jnp.sum(lv[0:1] * lv[1:2], axis=-1, keepdims=True))
           - jnp.exp(jnp.sum(lv[2:3] * lv[3:4], axis=-1, keepdims=True)) + lambda_init)
    a1 = acc_ref[0]
    a2 = acc_ref[1]
    o = a1[0:dv] / a1[dv:dv + 1] - lam * (a2[0:dv] / a2[dv:dv + 1])
    ms = jnp.mean(o * o, axis=0, keepdims=True)
    y = o * lax.rsqrt(ms + EPS) * gsub_ref[...]
    y = y * (1.0 - lambda_init)
    o_ref[0] = y.T.astype(o_ref.dtype)


def diff_attention(qkvx, lam_rows, g_sub, bsz, seq, lambda_init):
    tq = _tile(seq, 512)
    tk = tq // 2
    nq = seq // tq
    dv = DIFF_V_DIM
    kern = functools.partial(_diff_attn_kernel, tq=tq, tk=tk, lambda_init=lambda_init)
    return pl.pallas_call(
        kern,
        grid=(bsz, DIFF_HEADS, nq),
        in_specs=[pl.BlockSpec((4, DIFF_QK_DIM), lambda b, h, i: (0, 0)),
                  pl.BlockSpec((1, tq, dv), lambda b, h, i: (b, i, h)),
                  pl.BlockSpec((1, seq, dv), lambda b, h, i: (b, 0, DIFF_HEADS + h)),
                  pl.BlockSpec((1, seq, dv), lambda b, h, i: (b, 0, 2 * DIFF_HEADS + h)),
                  pl.BlockSpec((dv, 1), lambda b, h, i: (0, 0))],
        out_specs=pl.BlockSpec((1, tq, dv), lambda b, h, i: (b, i, h)),
        out_shape=jax.ShapeDtypeStruct((bsz, seq, DIFF_WIDTH), BF16),
        scratch_shapes=[pltpu.VMEM((dv + BF16_ROWS, seq), BF16),
                        pltpu.VMEM((2, tk, tq), F32),
                        pltpu.VMEM((GROUP, 2, tk, tq), BF16),
                        pltpu.VMEM((2, 1, tq), F32),
                        pltpu.VMEM((2, 1, tq), F32),
                        pltpu.VMEM((2, 1, tq), F32),
                        pltpu.VMEM((2, dv + BF16_ROWS, tq), F32)],
        compiler_params=_params(3),
        name="diff_attention",
    )(lam_rows, qkvx, qkvx, qkvx, g_sub.reshape(dv, 1))


def _cross_attn_kernel(xq_ref, kv_ref, o_ref):
    hd = CROSS_HEAD_DIM
    for h in range(CROSS_HEADS):
        q = xq_ref[0, :, h * hd:(h + 1) * hd]
        k = kv_ref[0, :, h * hd:(h + 1) * hd]
        v = kv_ref[0, :, CROSS_WIDTH + h * hd:CROSS_WIDTH + (h + 1) * hd]
        s = _dot_nt(q, k) * (hd ** -0.5)
        p = jnp.exp(s - jnp.max(s, axis=-1, keepdims=True))
        l = jnp.sum(p, axis=-1, keepdims=True)
        o_ref[0, :, h * hd:(h + 1) * hd] = (_dot(p.astype(BF16), v) / l).astype(o_ref.dtype)


def cross_attention(qkvx, kv, bsz, seq):
    tm = _tile(seq, 512)
    mem_len = kv.shape[1]
    return pl.pallas_call(
        _cross_attn_kernel,
        grid=(bsz, seq // tm),
        in_specs=[pl.BlockSpec((1, tm, CROSS_WIDTH), lambda b, i: (b, i, 3)),
                  pl.BlockSpec((1, mem_len, 2 * CROSS_WIDTH), lambda b, i: (b, 0, 0))],
        out_specs=pl.BlockSpec((1, tm, CROSS_WIDTH), lambda b, i: (b, i, 0)),
        out_shape=jax.ShapeDtypeStruct((bsz, seq, CROSS_WIDTH), BF16),
        compiler_params=_params(2),
        name="cross_attention",
    )(qkvx, kv)


def _merge_kernel(h_ref, ya_ref, yd_ref, yx_ref, wga_ref, wgb_ref, wgc_ref,
                  ba_ref, bb_ref, bc_ref, wa_ref, wd_ref, wx_ref, o_ref):
    h = h_ref[...]
    merged = jax.nn.sigmoid(_dot(h, wga_ref[...]) + ba_ref[...]) * _dot(ya_ref[...], wa_ref[...])
    merged += jax.nn.sigmoid(_dot(h, wgb_ref[...]) + bb_ref[...]) * _dot(yd_ref[...], wd_ref[...])
    merged += jax.nn.sigmoid(_dot(h, wgc_ref[...]) + bc_ref[...]) * _dot(yx_ref[...], wx_ref[...])
    o_ref[...] = merged.astype(o_ref.dtype)


def gated_merge(h, ya, yd, yx, w_gate, b_gate, w_a, w_d, w_x):
    m, d = h.shape
    tm = _tile(m, 1024)
    tn = 512
    nj = d // tn
    act = lambda width: pl.BlockSpec((tm, width), lambda i, j: (i, 0))
    gate_w = lambda br: pl.BlockSpec((d, tn), lambda i, j: (0, br * nj + j))
    gate_b = lambda br: pl.BlockSpec((1, tn), lambda i, j: (0, br * nj + j))
    out_w = lambda width: pl.BlockSpec((width, tn), lambda i, j: (0, j))
    return pl.pallas_call(
        _merge_kernel,
        grid=(m // tm, nj),
        in_specs=[act(d), act(CONV_WIDTH), act(DIFF_WIDTH), act(CROSS_WIDTH),
                  gate_w(0), gate_w(1), gate_w(2), gate_b(0), gate_b(1), gate_b(2),
                  out_w(CONV_WIDTH), out_w(DIFF_WIDTH), out_w(CROSS_WIDTH)],
        out_specs=pl.BlockSpec((tm, tn), lambda i, j: (i, j)),
        out_shape=jax.ShapeDtypeStruct((m, d), F32),
        compiler_params=_params(2),
        name="gated_merge",
    )(h, ya, yd, yx, w_gate, w_gate, w_gate, b_gate, b_gate, b_gate, w_a, w_d, w_x)


def _out_proj_kernel(z_ref, w_ref, x_ref, gpost_ref, gpre_ref, x1_ref, h2_ref):
    for r0 in range(0, z_ref.shape[0], EPILOGUE_ROWS):
        rows = slice(r0, r0 + EPILOGUE_ROWS)
        y = _dot(z_ref[rows, :], w_ref[...])
        x1 = x_ref[rows, :] + _rmsnorm_rows(y, gpost_ref[...])
        x1_ref[rows, :] = x1
        h2_ref[rows, :] = _rmsnorm_rows(x1, gpre_ref[...]).astype(h2_ref.dtype)


def out_proj_residual(z, w_o, x, g_post, g_next_pre):
    m, d = x.shape
    tm = _tile(m, 512)
    row = lambda: pl.BlockSpec((tm, d), lambda i: (i, 0))
    vec = lambda: pl.BlockSpec((1, d), lambda i: (0, 0))
    return pl.pallas_call(
        _out_proj_kernel,
        grid=(m // tm,),
        in_specs=[row(), pl.BlockSpec((d, d), lambda i: (0, 0)), row(), vec(), vec()],
        out_specs=[row(), row()],
        out_shape=[jax.ShapeDtypeStruct((m, d), F32), jax.ShapeDtypeStruct((m, d), F32)],
        compiler_params=_params(1),
        name="out_proj_residual",
    )(z, w_o, x, g_post.reshape(1, d), g_next_pre.reshape(1, d))


def _ffn_up_kernel(h_ref, wg_ref, wv_ref, cwg_ref, cwv_ref, o_ref, carry_g, carry_v, *, tiles_per_seq):
    i = pl.program_id(0)
    j = pl.program_id(1)
    start = i % tiles_per_seq == 0
    prev_g = _load_carry(carry_g, j, start)
    prev_v = _load_carry(carry_v, j, start)
    for r0 in range(0, h_ref.shape[0], FFN_UP_ROWS):
        h = h_ref[r0:r0 + FFN_UP_ROWS, :]
        zg = _dot(h, wg_ref[...])
        zv = _dot(h, wv_ref[...])
        gate = _causal_conv3(zg, prev_g, cwg_ref[...])
        val = _causal_conv3(zv, prev_v, cwv_ref[...])
        o_ref[r0:r0 + FFN_UP_ROWS, :] = (gate * jax.nn.sigmoid(gate) * val).astype(o_ref.dtype)
        prev_g = zg[FFN_UP_ROWS - SUBLANES:, :]
        prev_v = zv[FFN_UP_ROWS - SUBLANES:, :]
    carry_g[j] = prev_g
    carry_v[j] = prev_v


def ffn_up(h, w_up, conv_w, seq):
    m, d = h.shape
    d_ff = w_up.shape[1] // 2
    tm = _tile(seq, 1024)
    tn = 512
    nj = d_ff // tn
    kern = functools.partial(_ffn_up_kernel, tiles_per_seq=seq // tm)
    return pl.pallas_call(
        kern,
        grid=(m // tm, nj),
        in_specs=[pl.BlockSpec((tm, d), lambda i, j: (i, 0)),
                  pl.BlockSpec((d, tn), lambda i, j: (0, j)),
                  pl.BlockSpec((d, tn), lambda i, j: (0, nj + j)),
                  pl.BlockSpec((CONV_K, tn), lambda i, j: (0, j)),
                  pl.BlockSpec((CONV_K, tn), lambda i, j: (0, nj + j))],
        out_specs=pl.BlockSpec((tm, tn), lambda i, j: (i, j)),
        out_shape=jax.ShapeDtypeStruct((m, d_ff), BF16),
        scratch_shapes=[pltpu.VMEM((nj, SUBLANES, tn), F32), pltpu.VMEM((nj, SUBLANES, tn), F32)],
        compiler_params=_params(2),
        name="ffn_up",
    )(h, w_up, w_up, conv_w, conv_w)


def _ffn_down_kernel(a_ref, w_ref, x_ref, g_ref, o_ref):
    for r0 in range(0, a_ref.shape[0], EPILOGUE_ROWS):
        rows = slice(r0, r0 + EPILOGUE_ROWS)
        y = _dot(a_ref[rows, :], w_ref[...])
        o_ref[rows, :] = x_ref[rows, :] + _rmsnorm_rows(y, g_ref[...])


def ffn_down(a, w_down, x, g_post):
    m, d = x.shape
    d_ff = a.shape[1]
    tm = _tile(m, 512)
    row = lambda: pl.BlockSpec((tm, d), lambda i: (i, 0))
    return pl.pallas_call(
        _ffn_down_kernel,
        grid=(m // tm,),
        in_specs=[pl.BlockSpec((tm, d_ff), lambda i: (i, 0)),
                  pl.BlockSpec((d_ff, d), lambda i: (0, 0)),
                  row(), pl.BlockSpec((1, d), lambda i: (0, 0))],
        out_specs=row(),
        out_shape=jax.ShapeDtypeStruct((m, d), F32),
        compiler_params=_params(1),
        name="ffn_down",
    )(a, w_down, x, g_post.reshape(1, d))


def kernel(x, mem, g_mix_pre, w_in, w_gate, b_gate, conv_w, w_conv_out, lambda_q1, lambda_k1, lambda_q2, lambda_k2, g_diff_sub, w_diff_out, g_mem, w_mem_kv, w_cross_out, w_o, g_mix_post, g_ffn_pre, w_up, ffn_conv_w, w_down, g_ffn_post):
    bsz, seq, d = x.shape
    mem_len = mem.shape[1]
    depth = w_in.shape[0]
    m = bsz * seq
    q_lo = N_BRANCH * CONV_WIDTH
    qkvx_width = 2 * DIFF_QK_WIDTH + DIFF_WIDTH + CROSS_WIDTH
    qkvx_scale = jnp.where(jnp.arange(qkvx_width) < DIFF_QK_WIDTH,
                           LOG2E * DIFF_QK_DIM ** -0.5, 1.0).astype(F32)
    kv_scale = jnp.ones((2 * CROSS_WIDTH,), F32)

    xf = x.reshape(m, d)
    for l in range(depth):
        lambda_init = 0.8 - 0.6 * math.exp(-0.3 * l)
        w_in_b = w_in[l].astype(BF16)
        ya, h = conv_mixer(xf, g_mix_pre[l], w_in_b, conv_w[l], seq)
        qkvx = project(h, w_in_b, qkvx_scale, q_lo // 1024).reshape(bsz, seq, qkvx_width)

        lam_rows = jnp.stack([lambda_q1[l], lambda_k1[l], lambda_q2[l], lambda_k2[l]])
        yd = diff_attention(qkvx, lam_rows, g_diff_sub[l], bsz, seq, lambda_init)

        mem_n = rmsnorm_bf16(mem.reshape(bsz * mem_len, d), g_mem[l])
        kv = project(mem_n, w_mem_kv[l].astype(BF16), kv_scale, 0, tm_want=bsz * mem_len)
        yx = cross_attention(qkvx, kv.reshape(bsz, mem_len, 2 * CROSS_WIDTH), bsz, seq)

        merged = gated_merge(h, ya, yd.reshape(m, DIFF_WIDTH), yx.reshape(m, CROSS_WIDTH),
                             w_gate[l].astype(BF16), b_gate[l].reshape(1, -1),
                             w_conv_out[l].astype(BF16), w_diff_out[l].astype(BF16),
                             w_cross_out[l].astype(BF16))
        xf, h2 = out_proj_residual(merged, w_o[l].astype(BF16), xf, g_mix_post[l], g_ffn_pre[l])

        act = ffn_up(h2, w_up[l].astype(BF16), ffn_conv_w[l], seq)
        xf = ffn_down(act, w_down[l].astype(BF16), xf, g_ffn_post[l])
    return xf.reshape(bsz, seq, d)
```

```python
import functools
import math

import jax
import jax.numpy as jnp
from jax import lax
from jax.experimental import pallas as pl
from jax.experimental.pallas import tpu as pltpu

EPS = 1e-6
CHUNK = 64
CONV_WIDTH = 1024
CONV_K = 3
DIFF_HEADS = 8
DIFF_QK_DIM = 64
DIFF_V_DIM = 2 * DIFF_QK_DIM
DIFF_QK_WIDTH = DIFF_HEADS * 2 * DIFF_QK_DIM
DIFF_WIDTH = DIFF_HEADS * DIFF_V_DIM
CROSS_HEADS = 4
CROSS_HEAD_DIM = 256
CROSS_WIDTH = CROSS_HEADS * CROSS_HEAD_DIM
N_BRANCH = 3
LOG2E = math.log2(math.e)
MAX_EXCESS = 30.0
GROUP = 16

SUBLANES = 8
BF16_ROWS = 16
FFN_UP_ROWS = 128
EPILOGUE_ROWS = 256
VMEM_LIMIT = 56 * 1024 * 1024

F32 = jnp.float32
BF16 = jnp.bfloat16


def _params(n_axes):
    return pltpu.CompilerParams(
        dimension_semantics=("arbitrary",) * n_axes, vmem_limit_bytes=VMEM_LIMIT)


def _dot(a, b):
    return lax.dot_general(a, b, (((1,), (0,)), ((), ())), preferred_element_type=F32)


def _dot_nt(a, b):
    return lax.dot_general(a, b, (((1,), (1,)), ((), ())), preferred_element_type=F32)


def _tile(n, want):
    t = min(n, want)
    assert n % t == 0, (n, want)
    return t


def _rmsnorm_rows(x, g):
    ms = jnp.mean(x * x, axis=-1, keepdims=True)
    return x * lax.rsqrt(ms + EPS) * g


def _rmsnorm_kernel(x_ref, g_ref, o_ref):
    o_ref[...] = _rmsnorm_rows(x_ref[...], g_ref[...]).astype(o_ref.dtype)


def rmsnorm_bf16(x, g):
    m, d = x.shape
    tm = _tile(m, 512)
    return pl.pallas_call(
        _rmsnorm_kernel,
        grid=(m // tm,),
        in_specs=[pl.BlockSpec((tm, d), lambda i: (i, 0)),
                  pl.BlockSpec((1, d), lambda i: (0, 0))],
        out_specs=pl.BlockSpec((tm, d), lambda i: (i, 0)),
        out_shape=jax.ShapeDtypeStruct((m, d), BF16),
        compiler_params=_params(1),
        name="rmsnorm",
    )(x, g.reshape(1, d))


def _shift_rows(u, prev, s):
    rolled = pltpu.roll(u, s, axis=0)
    rprev = pltpu.roll(prev, s, axis=0)
    rows = lax.broadcasted_iota(jnp.int32, prev.shape, 0)
    first = jnp.where(rows < s, rprev, rolled[0:SUBLANES])
    return jnp.concatenate([first, rolled[SUBLANES:]], axis=0)


def _causal_conv3(u, prev, w):
    return (w[2:3] * u + w[1:2] * _shift_rows(u, prev, 1) + w[0:1] * _shift_rows(u, prev, 2))


def _load_carry(carry_ref, j, at_seq_start):
    @pl.when(at_seq_start)
    def _():
        carry_ref[j] = jnp.zeros(carry_ref.shape[1:], carry_ref.dtype)
    return carry_ref[j]


def _conv_mixer_kernel(x_ref, g_ref, wb_ref, wc_ref, wv_ref, cw_ref, o_ref, h_ref, carry_ref, *, tiles_per_seq):
    i = pl.program_id(0)
    j = pl.program_id(1)

    @pl.when(j == 0)
    def _():
        h_ref[...] = _rmsnorm_rows(x_ref[...], g_ref[...]).astype(h_ref.dtype)

    h = h_ref[...]
    u = _dot(h, wc_ref[...]) * _dot(h, wv_ref[...])
    prev = _load_carry(carry_ref, j, i % tiles_per_seq == 0)
    conv = _causal_conv3(u, prev, cw_ref[...])
    carry_ref[j] = u[u.shape[0] - SUBLANES:, :]
    o_ref[...] = (_dot(h, wb_ref[...]) * conv).astype(o_ref.dtype)


def conv_mixer(x, g_pre, w_in, conv_w, seq):
    m, d = x.shape
    tm = _tile(seq, 1024)
    tn = 512
    nj = CONV_WIDTH // tn
    kern = functools.partial(_conv_mixer_kernel, tiles_per_seq=seq // tm)
    return pl.pallas_call(
        kern,
        grid=(m // tm, nj),
        in_specs=[pl.BlockSpec((tm, d), lambda i, j: (i, 0)),
                  pl.BlockSpec((1, d), lambda i, j: (0, 0)),
                  pl.BlockSpec((d, tn), lambda i, j: (0, j)),
                  pl.BlockSpec((d, tn), lambda i, j: (0, nj + j)),
                  pl.BlockSpec((d, tn), lambda i, j: (0, 2 * nj + j)),
                  pl.BlockSpec((CONV_K, tn), lambda i, j: (0, j))],
        out_specs=[pl.BlockSpec((tm, tn), lambda i, j: (i, j)),
                   pl.BlockSpec((tm, d), lambda i, j: (i, 0))],
        out_shape=[jax.ShapeDtypeStruct((m, CONV_WIDTH), BF16), jax.ShapeDtypeStruct((m, d), BF16)],
        scratch_shapes=[pltpu.VMEM((nj, SUBLANES, tn), F32)],
        compiler_params=_params(2),
        name="conv_mixer",
    )(x, g_pre.reshape(1, d), w_in, w_in, w_in, conv_w)


def _proj_kernel(h_ref, w_ref, scale_ref, o_ref):
    o_ref[...] = (_dot(h_ref[...], w_ref[...]) * scale_ref[...]).astype(o_ref.dtype)


def project(h, w, col_scale, col_block0, tm_want=1024, tn=1024):
    m, d = h.shape
    n_out = col_scale.shape[0]
    tm = _tile(m, tm_want)
    return pl.pallas_call(
        _proj_kernel,
        grid=(m // tm, n_out // tn),
        in_specs=[pl.BlockSpec((tm, d), lambda i, j: (i, 0)),
                  pl.BlockSpec((d, tn), lambda i, j: (0, col_block0 + j)),
                  pl.BlockSpec((1, tn), lambda i, j: (0, j))],
        out_specs=pl.BlockSpec((tm, tn), lambda i, j: (i, j)),
        out_shape=jax.ShapeDtypeStruct((m, n_out), BF16),
        compiler_params=_params(2),
        name="project",
    )(h, w, col_scale.reshape(1, n_out))


def _col_max(s):
    r = s.shape[0]
    while r > SUBLANES:
        r //= 2
        s = jnp.maximum(s[:r], s[r:])
    return jnp.max(s, axis=0, keepdims=True)


def _diff_attn_kernel(lam_ref, q_ref, k_ref, v_ref, gsub_ref, o_ref,
                      vt_ref, s_ref, p_ref, alpha_ref, m_ref, excess_ref, acc_ref, *, tq, tk, lambda_init):
    qi = pl.program_id(2)
    seq = k_ref.shape[1]
    dv = DIFF_V_DIM
    assert tq == 2 * tk

    @pl.when(qi == 0)
    def _():
        vt_ref[dv:, :] = jnp.ones((BF16_ROWS, seq), BF16)

        def body(c, carry):
            r = pl.multiple_of(c * tq, tq)
            vt_ref[0:dv, pl.ds(r, tq)] = v_ref[0, pl.ds(r, tq), :].astype(F32).T.astype(BF16)
            return carry
        lax.fori_loop(0, seq // tq, body, 0)

    q = q_ref[0].astype(F32)
    lane = lax.broadcasted_iota(jnp.int32, q.shape, 1)
    qz = (jnp.where(lane < DIFF_QK_DIM, q, 0.0).astype(BF16),
          jnp.where(lane >= DIFF_QK_DIM, q, 0.0).astype(BF16))

    def k_tile(t):
        return k_ref[0, pl.ds(pl.multiple_of(t * tk, tk), tk), :]

    def vt_tile(t):
        return vt_ref[:, pl.ds(pl.multiple_of(t * tk, tk), tk)]

    key_chunk = lax.broadcasted_iota(jnp.int32, (tk, tq), 0) // CHUNK
    qry_chunk = lax.broadcasted_iota(jnp.int32, (tk, tq), 1) // CHUNK
    diag_masks = (key_chunk <= qry_chunk, key_chunk + tk // CHUNK <= qry_chunk)

    def exact_tile(t):
        k_t = k_tile(t)
        vt_t = vt_tile(t)
        for c in range(2):
            s = _dot_nt(k_t, qz[c])
            m_old = m_ref[c]
            m_new = jnp.maximum(m_old, _col_max(s))
            p = jnp.exp2(s - m_new).astype(BF16)
            acc_ref[c] = jnp.exp2(m_old - m_new) * acc_ref[c] + _dot(vt_t, p)
            m_ref[c] = m_new

    def exact_diagonal():
        k_d = [k_tile(2 * qi + d) for d in range(2)]
        vt_d = [vt_tile(2 * qi + d) for d in range(2)]
        for c in range(2):
            s = [jnp.where(diag_masks[d], _dot_nt(k_d[d], qz[c]), -jnp.inf) for d in range(2)]
            m = jnp.maximum(_col_max(s[0]), _col_max(s[1]))
            acc_ref[c] = (_dot(vt_d[0], jnp.exp2(s[0] - m).astype(BF16))
                          + _dot(vt_d[1], jnp.exp2(s[1] - m).astype(BF16)))
            m_ref[c] = m

    def fast_group(t0, n, masks=None):
        assert 2 <= n <= GROUP
        slots = range(GROUP - n, GROUP)
        m_frozen = [m_ref[c] for c in range(2)]
        tile_max = [None, None]
        if masks is None:
            k_next = k_tile(t0 + n)
            s_next = [_dot_nt(k_next, qz[c]) for c in range(2)]
        for c in range(2):
            s = s_ref[c]
            if masks is not None:
                s = jnp.where(masks[0], s, -jnp.inf)
            tile_max[c] = _col_max(s)
            p_ref[slots[0], c] = jnp.exp2(s - m_frozen[c]).astype(BF16)
            if masks is None:
                s_ref[c] = s_next[c]
        vt_pending = vt_tile(jnp.maximum(t0 - 1, 0))
        acc = [alpha_ref[c] * (acc_ref[c] + _dot(vt_pending, p_ref[GROUP - 1, c])) for c in range(2)]
        for i in range(1, n):
            k_t = k_tile(t0 + i)
            for c in range(2):
                s = _dot_nt(k_t, qz[c])
                if masks is not None:
                    s = jnp.where(masks[i], s, -jnp.inf)
                tile_max[c] = jnp.maximum(tile_max[c], _col_max(s))
                p_ref[slots[i], c] = jnp.exp2(s - m_frozen[c]).astype(BF16)
            vt_t = vt_tile(t0 + i - 1)
            for c in range(2):
                acc[c] = acc[c] + _dot(vt_t, p_ref[slots[i - 1], c])
        if masks is not None:
            vt_t = vt_tile(t0 + n - 1)
            for c in range(2):
                acc[c] = acc[c] + _dot(vt_t, p_ref[slots[n - 1], c])
        for c in range(2):
            acc_ref[c] = acc[c]
            m_new = jnp.maximum(m_frozen[c], tile_max[c])
            alpha_ref[c] = jnp.exp2(m_frozen[c] - m_new)
            excess_ref[c] = jnp.maximum(excess_ref[c], tile_max[c] - m_frozen[c])
            m_ref[c] = m_new

    for c in range(2):
        s0 = _dot_nt(k_tile(0), qz[c])
        s_ref[c] = s0
        m_ref[c] = _col_max(s0[0:CHUNK])
    acc_ref[...] = jnp.zeros(acc_ref.shape, F32)
    p_ref[GROUP - 1] = jnp.zeros(p_ref.shape[1:], BF16)
    alpha_ref[...] = jnp.ones(alpha_ref.shape, F32)
    excess_ref[...] = jnp.zeros(excess_ref.shape, F32)
    n_tiles = 2 * qi
    rem = n_tiles % GROUP
    size = 2
    while size < GROUP:
        @pl.when(rem & size > 0)
        def _(size=size):
            fast_group(rem & (size - 1), size)
        size *= 2

    def group(j, carry):
        fast_group(rem + GROUP * j, GROUP)
        return carry
    lax.fori_loop(0, n_tiles // GROUP, group, 0)
    fast_group(n_tiles, 2, diag_masks)

    worst = jnp.max(jnp.maximum(excess_ref[0], excess_ref[1]), axis=-1, keepdims=True)

    @pl.when(worst[0, 0] > MAX_EXCESS)
    def _():
        exact_diagonal()

        def body(t, carry):
            exact_tile(t)
            return carry
        lax.fori_loop(0, 2 * qi, body, 0)

    lv = lam_ref[...]
    lam = (jnp.exp(jnp.sum(lv[0:1] * lv[1:2], axis=-1, keepdims=True))
           - jnp.exp(jnp.sum(lv[2:3] * lv[3:4], axis=-1, keepdims=True)) + lambda_init)
    a1 = acc_ref[0]
    a2 = acc_ref[1]
    o = a1[0:dv] / a1[dv:dv + 1] - lam * (a2[0:dv] / a2[dv:dv + 1])
    ms = jnp.mean(o * o, axis=0, keepdims=True)
    y = o * lax.rsqrt(ms + EPS) * gsub_ref[...]
    y = y * (1.0 - lambda_init)
    o_ref[0] = y.T.astype(o_ref.dtype)


def diff_attention(qkvx, lam_rows, g_sub, bsz, seq, lambda_init):
    tq = _tile(seq, 512)
    tk = tq // 2
    nq = seq // tq
    dv = DIFF_V_DIM
    kern = functools.partial(_diff_attn_kernel, tq=tq, tk=tk, lambda_init=lambda_init)
    return pl.pallas_call(
        kern,
        grid=(bsz, DIFF_HEADS, nq),
        in_specs=[pl.BlockSpec((4, DIFF_QK_DIM), lambda b, h, i: (0, 0)),
                  pl.BlockSpec((1, tq, dv), lambda b, h, i: (b, i, h)),
                  pl.BlockSpec((1, seq, dv), lambda b, h, i: (b, 0, DIFF_HEADS + h)),
                  pl.BlockSpec((1, seq, dv), lambda b, h, i: (b, 0, 2 * DIFF_HEADS + h)),
                  pl.BlockSpec((dv, 1), lambda b, h, i: (0, 0))],
        out_specs=pl.BlockSpec((1, tq, dv), lambda b, h, i: (b, i, h)),
        out_shape=jax.ShapeDtypeStruct((bsz, seq, DIFF_WIDTH), BF16),
        scratch_shapes=[pltpu.VMEM((dv + BF16_ROWS, seq), BF16),
                        pltpu.VMEM((2, tk, tq), F32),
                        pltpu.VMEM((GROUP, 2, tk, tq), BF16),
                        pltpu.VMEM((2, 1, tq), F32),
                        pltpu.VMEM((2, 1, tq), F32),
                        pltpu.VMEM((2, 1, tq), F32),
                        pltpu.VMEM((2, dv + BF16_ROWS, tq), F32)],
        compiler_params=_params(3),
        name="diff_attention",
    )(lam_rows, qkvx, qkvx, qkvx, g_sub.reshape(dv, 1))


def _cross_attn_kernel(xq_ref, kv_ref, o_ref):
    hd = CROSS_HEAD_DIM
    for h in range(CROSS_HEADS):
        q = xq_ref[0, :, h * hd:(h + 1) * hd]
        k = kv_ref[0, :, h * hd:(h + 1) * hd]
        v = kv_ref[0, :, CROSS_WIDTH + h * hd:CROSS_WIDTH + (h + 1) * hd]
        s = _dot_nt(q, k) * (hd ** -0.5)
        p = jnp.exp(s - jnp.max(s, axis=-1, keepdims=True))
        l = jnp.sum(p, axis=-1, keepdims=True)
        o_ref[0, :, h * hd:(h + 1) * hd] = (_dot(p.astype(BF16), v) / l).astype(o_ref.dtype)


def cross_attention(qkvx, kv, bsz, seq):
    tm = _tile(seq, 512)
    mem_len = kv.shape[1]
    return pl.pallas_call(
        _cross_attn_kernel,
        grid=(bsz, seq // tm),
        in_specs=[pl.BlockSpec((1, tm, CROSS_WIDTH), lambda b, i: (b, i, 3)),
                  pl.BlockSpec((1, mem_len, 2 * CROSS_WIDTH), lambda b, i: (b, 0, 0))],
        out_specs=pl.BlockSpec((1, tm, CROSS_WIDTH), lambda b, i: (b, i, 0)),
        out_shape=jax.ShapeDtypeStruct((bsz, seq, CROSS_WIDTH), BF16),
        compiler_params=_params(2),
        name="cross_attention",
    )(qkvx, kv)


def _merge_kernel(h_ref, ya_ref, yd_ref, yx_ref, wga_ref, wgb_ref, wgc_ref,
                  ba_ref, bb_ref, bc_ref, wa_ref, wd_ref, wx_ref, o_ref):
    h = h_ref[...]
    merged = jax.nn.sigmoid(_dot(h, wga_ref[...]) + ba_ref[...]) * _dot(ya_ref[...], wa_ref[...])
    merged += jax.nn.sigmoid(_dot(h, wgb_ref[...]) + bb_ref[...]) * _dot(yd_ref[...], wd_ref[...])
    merged += jax.nn.sigmoid(_dot(h, wgc_ref[...]) + bc_ref[...]) * _dot(yx_ref[...], wx_ref[...])
    o_ref[...] = merged.astype(o_ref.dtype)


def gated_merge(h, ya, yd, yx, w_gate, b_gate, w_a, w_d, w_x):
    m, d = h.shape
    tm = _tile(m, 1024)
    tn = 512
    nj = d // tn
    act = lambda width: pl.BlockSpec((tm, width), lambda i, j: (i, 0))
    gate_w = lambda br: pl.BlockSpec((d, tn), lambda i, j: (0, br * nj + j))
    gate_b = lambda br: pl.BlockSpec((1, tn), lambda i, j: (0, br * nj + j))
    out_w = lambda width: pl.BlockSpec((width, tn), lambda i, j: (0, j))
    return pl.pallas_call(
        _merge_kernel,
        grid=(m // tm, nj),
        in_specs=[act(d), act(CONV_WIDTH), act(DIFF_WIDTH), act(CROSS_WIDTH),
                  gate_w(0), gate_w(1), gate_w(2), gate_b(0), gate_b(1), gate_b(2),
                  out_w(CONV_WIDTH), out_w(DIFF_WIDTH), out_w(CROSS_WIDTH)],
        out_specs=pl.BlockSpec((tm, tn), lambda i, j: (i, j)),
        out_shape=jax.ShapeDtypeStruct((m, d), BF16),
        compiler_params=_params(2),
        name="gated_merge",
    )(h, ya, yd, yx, w_gate, w_gate, w_gate, b_gate, b_gate, b_gate, w_a, w_d, w_x)


def _out_proj_kernel(z_ref, w_ref, x_ref, gpost_ref, gpre_ref, x1_ref, h2_ref):
    for r0 in range(0, z_ref.shape[0], EPILOGUE_ROWS):
        rows = slice(r0, r0 + EPILOGUE_ROWS)
        y = _dot(z_ref[rows, :], w_ref[...])
        x1 = x_ref[rows, :] + _rmsnorm_rows(y, gpost_ref[...])
        x1_ref[rows, :] = x1
        h2_ref[rows, :] = _rmsnorm_rows(x1, gpre_ref[...]).astype(h2_ref.dtype)


def out_proj_residual(z, w_o, x, g_post, g_next_pre):
    m, d = x.shape
    tm = _tile(m, 512)
    row = lambda: pl.BlockSpec((tm, d), lambda i: (i, 0))
    vec = lambda: pl.BlockSpec((1, d), lambda i: (0, 0))
    return pl.pallas_call(
        _out_proj_kernel,
        grid=(m // tm,),
        in_specs=[row(), pl.BlockSpec((d, d), lambda i: (0, 0)), row(), vec(), vec()],
        out_specs=[row(), row()],
        out_shape=[jax.ShapeDtypeStruct((m, d), F32), jax.ShapeDtypeStruct((m, d), F32)],
        compiler_params=_params(1),
        name="out_proj_residual",
    )(z, w_o, x, g_post.reshape(1, d), g_next_pre.reshape(1, d))


def _ffn_up_kernel(h_ref, wg_ref, wv_ref, cwg_ref, cwv_ref, o_ref, carry_g, carry_v, *, tiles_per_seq):
    i = pl.program_id(0)
    j = pl.program_id(1)
    start = i % tiles_per_seq == 0
    prev_g = _load_carry(carry_g, j, start)
    prev_v = _load_carry(carry_v, j, start)
    for r0 in range(0, h_ref.shape[0], FFN_UP_ROWS):
        h = h_ref[r0:r0 + FFN_UP_ROWS, :]
        zg = _dot(h, wg_ref[...])
        zv = _dot(h, wv_ref[...])
        gate = _causal_conv3(zg, prev_g, cwg_ref[...])
        val = _causal_conv3(zv, prev_v, cwv_ref[...])
        o_ref[r0:r0 + FFN_UP_ROWS, :] = (gate * jax.nn.sigmoid(gate) * val).astype(o_ref.dtype)
        prev_g = zg[FFN_UP_ROWS - SUBLANES:, :]
        prev_v = zv[FFN_UP_ROWS - SUBLANES:, :]
    carry_g[j] = prev_g
    carry_v[j] = prev_v


def ffn_up(h, w_up, conv_w, seq):
    m, d = h.shape
    d_ff = w_up.shape[1] // 2
    tm = _tile(seq, 1024)
    tn = 512
    nj = d_ff // tn
    kern = functools.partial(_ffn_up_kernel, tiles_per_seq=seq // tm)
    return pl.pallas_call(
        kern,
        grid=(m // tm, nj),
        in_specs=[pl.BlockSpec((tm, d), lambda i, j: (i, 0)),
                  pl.BlockSpec((d, tn), lambda i, j: (0, j)),
                  pl.BlockSpec((d, tn), lambda i, j: (0, nj + j)),
                  pl.BlockSpec((CONV_K, tn), lambda i, j: (0, j)),
                  pl.BlockSpec((CONV_K, tn), lambda i, j: (0, nj + j))],
        out_specs=pl.BlockSpec((tm, tn), lambda i, j: (i, j)),
        out_shape=jax.ShapeDtypeStruct((m, d_ff), BF16),
        scratch_shapes=[pltpu.VMEM((nj, SUBLANES, tn), F32), pltpu.VMEM((nj, SUBLANES, tn), F32)],
        compiler_params=_params(2),
        name="ffn_up",
    )(h, w_up, w_up, conv_w, conv_w)


def _ffn_down_kernel(a_ref, w_ref, x_ref, g_ref, o_ref):
    for r0 in range(0, a_ref.shape[0], EPILOGUE_ROWS):
        rows = slice(r0, r0 + EPILOGUE_ROWS)
        y = _dot(a_ref[rows, :], w_ref[...])
        o_ref[rows, :] = x_ref[rows, :] + _rmsnorm_rows(y, g_ref[...])


def ffn_down(a, w_down, x, g_post):
    m, d = x.shape
    d_ff = a.shape[1]
    tm = _tile(m, 512)
    row = lambda: pl.BlockSpec((tm, d), lambda i: (i, 0))
    return pl.pallas_call(
        _ffn_down_kernel,
        grid=(m // tm,),
        in_specs=[pl.BlockSpec((tm, d_ff), lambda i: (i, 0)),
                  pl.BlockSpec((d_ff, d), lambda i: (0, 0)),
                  row(), pl.BlockSpec((1, d), lambda i: (0, 0))],
        out_specs=row(),
        out_shape=jax.ShapeDtypeStruct((m, d), F32),
        compiler_params=_params(1),
        name="ffn_down",
    )(a, w_down, x, g_post.reshape(1, d))


def kernel(x, mem, g_mix_pre, w_in, w_gate, b_gate, conv_w, w_conv_out, lambda_q1, lambda_k1, lambda_q2, lambda_k2, g_diff_sub, w_diff_out, g_mem, w_mem_kv, w_cross_out, w_o, g_mix_post, g_ffn_pre, w_up, ffn_conv_w, w_down, g_ffn_post):
    bsz, seq, d = x.shape
    mem_len = mem.shape[1]
    depth = w_in.shape[0]
    m = bsz * seq
    q_lo = N_BRANCH * CONV_WIDTH
    qkvx_width = 2 * DIFF_QK_WIDTH + DIFF_WIDTH + CROSS_WIDTH
    qkvx_scale = jnp.where(jnp.arange(qkvx_width) < DIFF_QK_WIDTH,
                           LOG2E * DIFF_QK_DIM ** -0.5, 1.0).astype(F32)
    kv_scale = jnp.ones((2 * CROSS_WIDTH,), F32)

    xf = x.reshape(m, d)
    for l in range(depth):
        lambda_init = 0.8 - 0.6 * math.exp(-0.3 * l)
        w_in_b = w_in[l].astype(BF16)
        ya, h = conv_mixer(xf, g_mix_pre[l], w_in_b, conv_w[l], seq)
        qkvx = project(h, w_in_b, qkvx_scale, q_lo // 1024).reshape(bsz, seq, qkvx_width)

        lam_rows = jnp.stack([lambda_q1[l], lambda_k1[l], lambda_q2[l], lambda_k2[l]])
        yd = diff_attention(qkvx, lam_rows, g_diff_sub[l], bsz, seq, lambda_init)

        mem_n = rmsnorm_bf16(mem.reshape(bsz * mem_len, d), g_mem[l])
        kv = project(mem_n, w_mem_kv[l].astype(BF16), kv_scale, 0, tm_want=bsz * mem_len)
        yx = cross_attention(qkvx, kv.reshape(bsz, mem_len, 2 * CROSS_WIDTH), bsz, seq)

        merged = gated_merge(h, ya, yd.reshape(m, DIFF_WIDTH), yx.reshape(m, CROSS_WIDTH),
                             w_gate[l].astype(BF16), b_gate[l].reshape(1, -1),
                             w_conv_out[l].astype(BF16), w_diff_out[l].astype(BF16),
                             w_cross_out[l].astype(BF16))
        xf, h2 = out_proj_residual(merged, w_o[l].astype(BF16), xf, g_mix_post[l], g_ffn_pre[l])

        act = ffn_up(h2, w_up[l].astype(BF16), ffn_conv_w[l], seq)
        xf = ffn_down(act, w_down[l].astype(BF16), xf, g_ffn_post[l])
    return xf.reshape(bsz, seq, d)
```

```python
import functools
import math

import jax
import jax.numpy as jnp
from jax import lax
from jax.experimental import pallas as pl
from jax.experimental.pallas import tpu as pltpu

EPS = 1e-6
CHUNK = 64
CONV_WIDTH = 1024
CONV_K = 3
DIFF_HEADS = 8
DIFF_QK_DIM = 64
DIFF_V_DIM = 2 * DIFF_QK_DIM
DIFF_QK_WIDTH = DIFF_HEADS * 2 * DIFF_QK_DIM
DIFF_WIDTH = DIFF_HEADS * DIFF_V_DIM
CROSS_HEADS = 4
CROSS_HEAD_DIM = 256
CROSS_WIDTH = CROSS_HEADS * CROSS_HEAD_DIM
N_BRANCH = 3
LOG2E = math.log2(math.e)
MAX_EXCESS = 30.0
GROUP = 32

SUBLANES = 8
BF16_ROWS = 16
FFN_UP_ROWS = 128
EPILOGUE_ROWS = 256
VMEM_LIMIT = 56 * 1024 * 1024

F32 = jnp.float32
BF16 = jnp.bfloat16


def _params(n_axes):
    return pltpu.CompilerParams(
        dimension_semantics=("arbitrary",) * n_axes, vmem_limit_bytes=VMEM_LIMIT)


def _dot(a, b):
    return lax.dot_general(a, b, (((1,), (0,)), ((), ())), preferred_element_type=F32)


def _dot_nt(a, b):
    return lax.dot_general(a, b, (((1,), (1,)), ((), ())), preferred_element_type=F32)


def _tile(n, want):
    t = min(n, want)
    assert n % t == 0, (n, want)
    return t


def _rmsnorm_rows(x, g):
    ms = jnp.mean(x * x, axis=-1, keepdims=True)
    return x * lax.rsqrt(ms + EPS) * g


def _rmsnorm_kernel(x_ref, g_ref, o_ref):
    o_ref[...] = _rmsnorm_rows(x_ref[...], g_ref[...]).astype(o_ref.dtype)


def rmsnorm_bf16(x, g):
    m, d = x.shape
    tm = _tile(m, 512)
    return pl.pallas_call(
        _rmsnorm_kernel,
        grid=(m // tm,),
        in_specs=[pl.BlockSpec((tm, d), lambda i: (i, 0)),
                  pl.BlockSpec((1, d), lambda i: (0, 0))],
        out_specs=pl.BlockSpec((tm, d), lambda i: (i, 0)),
        out_shape=jax.ShapeDtypeStruct((m, d), BF16),
        compiler_params=_params(1),
        name="rmsnorm",
    )(x, g.reshape(1, d))


def _shift_rows(u, prev, s):
    rolled = pltpu.roll(u, s, axis=0)
    rprev = pltpu.roll(prev, s, axis=0)
    rows = lax.broadcasted_iota(jnp.int32, prev.shape, 0)
    first = jnp.where(rows < s, rprev, rolled[0:SUBLANES])
    return jnp.concatenate([first, rolled[SUBLANES:]], axis=0)


def _causal_conv3(u, prev, w):
    return (w[2:3] * u + w[1:2] * _shift_rows(u, prev, 1) + w[0:1] * _shift_rows(u, prev, 2))


def _load_carry(carry_ref, j, at_seq_start):
    @pl.when(at_seq_start)
    def _():
        carry_ref[j] = jnp.zeros(carry_ref.shape[1:], carry_ref.dtype)
    return carry_ref[j]


def _conv_mixer_kernel(x_ref, g_ref, wb_ref, wc_ref, wv_ref, cw_ref, o_ref, h_ref, carry_ref, *, tiles_per_seq):
    i = pl.program_id(0)
    j = pl.program_id(1)

    @pl.when(j == 0)
    def _():
        h_ref[...] = _rmsnorm_rows(x_ref[...], g_ref[...]).astype(h_ref.dtype)

    h = h_ref[...]
    u = _dot(h, wc_ref[...]) * _dot(h, wv_ref[...])
    prev = _load_carry(carry_ref, j, i % tiles_per_seq == 0)
    conv = _causal_conv3(u, prev, cw_ref[...])
    carry_ref[j] = u[u.shape[0] - SUBLANES:, :]
    o_ref[...] = (_dot(h, wb_ref[...]) * conv).astype(o_ref.dtype)


def conv_mixer(x, g_pre, w_in, conv_w, seq):
    m, d = x.shape
    tm = _tile(seq, 1024)
    tn = 512
    nj = CONV_WIDTH // tn
    kern = functools.partial(_conv_mixer_kernel, tiles_per_seq=seq // tm)
    return pl.pallas_call(
        kern,
        grid=(m // tm, nj),
        in_specs=[pl.BlockSpec((tm, d), lambda i, j: (i, 0)),
                  pl.BlockSpec((1, d), lambda i, j: (0, 0)),
                  pl.BlockSpec((d, tn), lambda i, j: (0, j)),
                  pl.BlockSpec((d, tn), lambda i, j: (0, nj + j)),
                  pl.BlockSpec((d, tn), lambda i, j: (0, 2 * nj + j)),
                  pl.BlockSpec((CONV_K, tn), lambda i, j: (0, j))],
        out_specs=[pl.BlockSpec((tm, tn), lambda i, j: (i, j)),
                   pl.BlockSpec((tm, d), lambda i, j: (i, 0))],
        out_shape=[jax.ShapeDtypeStruct((m, CONV_WIDTH), BF16), jax.ShapeDtypeStruct((m, d), BF16)],
        scratch_shapes=[pltpu.VMEM((nj, SUBLANES, tn), F32)],
        compiler_params=_params(2),
        name="conv_mixer",
    )(x, g_pre.reshape(1, d), w_in, w_in, w_in, conv_w)


def _proj_kernel(h_ref, w_ref, scale_ref, o_ref):
    o_ref[...] = (_dot(h_ref[...], w_ref[...]) * scale_ref[...]).astype(o_ref.dtype)


def project(h, w, col_scale, col_block0, tm_want=1024, tn=1024):
    m, d = h.shape
    n_out = col_scale.shape[0]
    tm = _tile(m, tm_want)
    return pl.pallas_call(
        _proj_kernel,
        grid=(m // tm, n_out // tn),
        in_specs=[pl.BlockSpec((tm, d), lambda i, j: (i, 0)),
                  pl.BlockSpec((d, tn), lambda i, j: (0, col_block0 + j)),
                  pl.BlockSpec((1, tn), lambda i, j: (0, j))],
        out_specs=pl.BlockSpec((tm, tn), lambda i, j: (i, j)),
        out_shape=jax.ShapeDtypeStruct((m, n_out), BF16),
        compiler_params=_params(2),
        name="project",
    )(h, w, col_scale.reshape(1, n_out))


def _col_max(s):
    r = s.shape[0]
    while r > SUBLANES:
        r //= 2
        s = jnp.maximum(s[:r], s[r:])
    return jnp.max(s, axis=0, keepdims=True)


def _diff_attn_kernel(lam_ref, q_ref, k_ref, v_ref, gsub_ref, o_ref,
                      vt_ref, s_ref, p_ref, alpha_ref, m_ref, excess_ref, acc_ref, *, tq, tk, lambda_init):
    qi = pl.program_id(2)
    seq = k_ref.shape[1]
    dv = DIFF_V_DIM
    assert tq == 2 * tk

    @pl.when(qi == 0)
    def _():
        vt_ref[dv:, :] = jnp.ones((BF16_ROWS, seq), BF16)

        def body(c, carry):
            r = pl.multiple_of(c * tq, tq)
            vt_ref[0:dv, pl.ds(r, tq)] = v_ref[0, pl.ds(r, tq), :].astype(F32).T.astype(BF16)
            return carry
        lax.fori_loop(0, seq // tq, body, 0)

    q = q_ref[0].astype(F32)
    lane = lax.broadcasted_iota(jnp.int32, q.shape, 1)
    qz = (jnp.where(lane < DIFF_QK_DIM, q, 0.0).astype(BF16),
          jnp.where(lane >= DIFF_QK_DIM, q, 0.0).astype(BF16))

    def k_tile(t):
        return k_ref[0, pl.ds(pl.multiple_of(t * tk, tk), tk), :]

    def vt_tile(t):
        return vt_ref[:, pl.ds(pl.multiple_of(t * tk, tk), tk)]

    key_chunk = lax.broadcasted_iota(jnp.int32, (tk, tq), 0) // CHUNK
    qry_chunk = lax.broadcasted_iota(jnp.int32, (tk, tq), 1) // CHUNK
    diag_masks = (key_chunk <= qry_chunk, key_chunk + tk // CHUNK <= qry_chunk)

    def exact_tile(t):
        k_t = k_tile(t)
        vt_t = vt_tile(t)
        for c in range(2):
            s = _dot_nt(k_t, qz[c])
            m_old = m_ref[c]
            m_new = jnp.maximum(m_old, _col_max(s))
            p = jnp.exp2(s - m_new).astype(BF16)
            acc_ref[c] = jnp.exp2(m_old - m_new) * acc_ref[c] + _dot(vt_t, p)
            m_ref[c] = m_new

    def exact_diagonal():
        k_d = [k_tile(2 * qi + d) for d in range(2)]
        vt_d = [vt_tile(2 * qi + d) for d in range(2)]
        for c in range(2):
            s = [jnp.where(diag_masks[d], _dot_nt(k_d[d], qz[c]), -jnp.inf) for d in range(2)]
            m = jnp.maximum(_col_max(s[0]), _col_max(s[1]))
            acc_ref[c] = (_dot(vt_d[0], jnp.exp2(s[0] - m).astype(BF16))
                          + _dot(vt_d[1], jnp.exp2(s[1] - m).astype(BF16)))
            m_ref[c] = m

    def fast_group(t0, n, masks=None):
        assert 2 <= n <= GROUP
        slots = range(GROUP - n, GROUP)
        m_frozen = [m_ref[c] for c in range(2)]
        tile_max = [None, None]
        if masks is None:
            k_next = k_tile(t0 + n)
            s_next = [_dot_nt(k_next, qz[c]) for c in range(2)]
        for c in range(2):
            s = s_ref[c]
            if masks is not None:
                s = jnp.where(masks[0], s, -jnp.inf)
            tile_max[c] = _col_max(s)
            p_ref[slots[0], c] = jnp.exp2(s - m_frozen[c]).astype(BF16)
            if masks is None:
                s_ref[c] = s_next[c]
        vt_pending = vt_tile(jnp.maximum(t0 - 1, 0))
        acc = [alpha_ref[c] * (acc_ref[c] + _dot(vt_pending, p_ref[GROUP - 1, c])) for c in range(2)]
        for i in range(1, n):
            k_t = k_tile(t0 + i)
            for c in range(2):
                s = _dot_nt(k_t, qz[c])
                if masks is not None:
                    s = jnp.where(masks[i], s, -jnp.inf)
                tile_max[c] = jnp.maximum(tile_max[c], _col_max(s))
                p_ref[slots[i], c] = jnp.exp2(s - m_frozen[c]).astype(BF16)
            vt_t = vt_tile(t0 + i - 1)
            for c in range(2):
                acc[c] = acc[c] + _dot(vt_t, p_ref[slots[i - 1], c])
        if masks is not None:
            vt_t = vt_tile(t0 + n - 1)
            for c in range(2):
                acc[c] = acc[c] + _dot(vt_t, p_ref[slots[n - 1], c])
        for c in range(2):
            acc_ref[c] = acc[c]
            m_new = jnp.maximum(m_frozen[c], tile_max[c])
            alpha_ref[c] = jnp.exp2(m_frozen[c] - m_new)
            excess_ref[c] = jnp.maximum(excess_ref[c], tile_max[c] - m_frozen[c])
            m_ref[c] = m_new

    for c in range(2):
        s0 = _dot_nt(k_tile(0), qz[c])
        s_ref[c] = s0
        m_ref[c] = _col_max(s0[0:CHUNK])
    acc_ref[...] = jnp.zeros(acc_ref.shape, F32)
    p_ref[GROUP - 1] = jnp.zeros(p_ref.shape[1:], BF16)
    alpha_ref[...] = jnp.ones(alpha_ref.shape, F32)
    excess_ref[...] = jnp.zeros(excess_ref.shape, F32)
    n_tiles = 2 * qi
    rem = n_tiles % GROUP
    size = 2
    while size < GROUP:
        @pl.when(rem & size > 0)
        def _(size=size):
            fast_group(rem & (size - 1), size)
        size *= 2

    def group(j, carry):
        fast_group(rem + GROUP * j, GROUP)
        return carry
    lax.fori_loop(0, n_tiles // GROUP, group, 0)
    fast_group(n_tiles, 2, diag_masks)

    worst = jnp.max(jnp.maximum(excess_ref[0], excess_ref[1]), axis=-1, keepdims=True)

    @pl.when(worst[0, 0] > MAX_EXCESS)
    def _():
        exact_diagonal()

        def body(t, carry):
            exact_tile(t)
            return carry
        lax.fori_loop(0, 2 * qi, body, 0)

    lv = lam_ref[...]
    lam = (jnp.exp(jnp.sum(lv[0:1] * lv[1:2], axis=-1, keepdims=True))
           - jnp.exp(jnp.sum(lv[2:3] * lv[3:4], axis=-1, keepdims=True)) + lambda_init)
    a1 = acc_ref[0]
    a2 = acc_ref[1]
    o = a1[0:dv] / a1[dv:dv + 1] - lam * (a2[0:dv] / a2[dv:dv + 1])
    ms = jnp.mean(o * o, axis=0, keepdims=True)
    y = o * lax.rsqrt(ms + EPS) * gsub_ref[...]
    y = y * (1.0 - lambda_init)
    o_ref[0] = y.T.astype(o_ref.dtype)


def diff_attention(qkvx, lam_rows, g_sub, bsz, seq, lambda_init):
    tq = _tile(seq, 512)
    tk = tq // 2
    nq = seq // tq
    dv = DIFF_V_DIM
    kern = functools.partial(_diff_attn_kernel, tq=tq, tk=tk, lambda_init=lambda_init)
    return pl.pallas_call(
        kern,
        grid=(bsz, DIFF_HEADS, nq),
        in_specs=[pl.BlockSpec((4, DIFF_QK_DIM), lambda b, h, i: (0, 0)),
                  pl.BlockSpec((1, tq, dv), lambda b, h, i: (b, i, h)),
                  pl.BlockSpec((1, seq, dv), lambda b, h, i: (b, 0, DIFF_HEADS + h)),
                  pl.BlockSpec((1, seq, dv), lambda b, h, i: (b, 0, 2 * DIFF_HEADS + h)),
                  pl.BlockSpec((dv, 1), lambda b, h, i: (0, 0))],
        out_specs=pl.BlockSpec((1, tq, dv), lambda b, h, i: (b, i, h)),
        out_shape=jax.ShapeDtypeStruct((bsz, seq, DIFF_WIDTH), BF16),
        scratch_shapes=[pltpu.VMEM((dv + BF16_ROWS, seq), BF16),
                        pltpu.VMEM((2, tk, tq), F32),
                        pltpu.VMEM((GROUP, 2, tk, tq), BF16),
                        pltpu.VMEM((2, 1, tq), F32),
                        pltpu.VMEM((2, 1, tq), F32),
                        pltpu.VMEM((2, 1, tq), F32),
                        pltpu.VMEM((2, dv + BF16_ROWS, tq), F32)],
        compiler_params=_params(3),
        name="diff_attention",
    )(lam_rows, qkvx, qkvx, qkvx, g_sub.reshape(dv, 1))


def _cross_attn_kernel(xq_ref, kv_ref, o_ref):
    hd = CROSS_HEAD_DIM
    for h in range(CROSS_HEADS):
        q = xq_ref[0, :, h * hd:(h + 1) * hd]
        k = kv_ref[0, :, h * hd:(h + 1) * hd]
        v = kv_ref[0, :, CROSS_WIDTH + h * hd:CROSS_WIDTH + (h + 1) * hd]
        s = _dot_nt(q, k) * (hd ** -0.5)
        p = jnp.exp(s - jnp.max(s, axis=-1, keepdims=True))
        l = jnp.sum(p, axis=-1, keepdims=True)
        o_ref[0, :, h * hd:(h + 1) * hd] = (_dot(p.astype(BF16), v) / l).astype(o_ref.dtype)


def cross_attention(qkvx, kv, bsz, seq):
    tm = _tile(seq, 512)
    mem_len = kv.shape[1]
    return pl.pallas_call(
        _cross_attn_kernel,
        grid=(bsz, seq // tm),
        in_specs=[pl.BlockSpec((1, tm, CROSS_WIDTH), lambda b, i: (b, i, 3)),
                  pl.BlockSpec((1, mem_len, 2 * CROSS_WIDTH), lambda b, i: (b, 0, 0))],
        out_specs=pl.BlockSpec((1, tm, CROSS_WIDTH), lambda b, i: (b, i, 0)),
        out_shape=jax.ShapeDtypeStruct((bsz, seq, CROSS_WIDTH), BF16),
        compiler_params=_params(2),
        name="cross_attention",
    )(qkvx, kv)


def _merge_kernel(h_ref, ya_ref, yd_ref, yx_ref, wga_ref, wgb_ref, wgc_ref,
                  ba_ref, bb_ref, bc_ref, wa_ref, wd_ref, wx_ref, o_ref):
    h = h_ref[...]
    merged = jax.nn.sigmoid(_dot(h, wga_ref[...]) + ba_ref[...]) * _dot(ya_ref[...], wa_ref[...])
    merged += jax.nn.sigmoid(_dot(h, wgb_ref[...]) + bb_ref[...]) * _dot(yd_ref[...], wd_ref[...])
    merged += jax.nn.sigmoid(_dot(h, wgc_ref[...]) + bc_ref[...]) * _dot(yx_ref[...], wx_ref[...])
    o_ref[...] = merged.astype(o_ref.dtype)


def gated_merge(h, ya, yd, yx, w_gate, b_gate, w_a, w_d, w_x):
    m, d = h.shape
    tm = _tile(m, 1024)
    tn = 512
    nj = d // tn
    act = lambda width: pl.BlockSpec((tm, width), lambda i, j: (i, 0))
    gate_w = lambda br: pl.BlockSpec((d, tn), lambda i, j: (0, br * nj + j))
    gate_b = lambda br: pl.BlockSpec((1, tn), lambda i, j: (0, br * nj + j))
    out_w = lambda width: pl.BlockSpec((width, tn), lambda i, j: (0, j))
    return pl.pallas_call(
        _merge_kernel,
        grid=(m // tm, nj),
        in_specs=[act(d), act(CONV_WIDTH), act(DIFF_WIDTH), act(CROSS_WIDTH),
                  gate_w(0), gate_w(1), gate_w(2), gate_b(0), gate_b(1), gate_b(2),
                  out_w(CONV_WIDTH), out_w(DIFF_WIDTH), out_w(CROSS_WIDTH)],
        out_specs=pl.BlockSpec((tm, tn), lambda i, j: (i, j)),
        out_shape=jax.ShapeDtypeStruct((m, d), BF16),
        compiler_params=_params(2),
        name="gated_merge",
    )(h, ya, yd, yx, w_gate, w_gate, w_gate, b_gate, b_gate, b_gate, w_a, w_d, w_x)


def _out_proj_kernel(z_ref, w_ref, x_ref, gpost_ref, gpre_ref, x1_ref, h2_ref):
    for r0 in range(0, z_ref.shape[0], EPILOGUE_ROWS):
        rows = slice(r0, r0 + EPILOGUE_ROWS)
        y = _dot(z_ref[rows, :], w_ref[...])
        x1 = x_ref[rows, :] + _rmsnorm_rows(y, gpost_ref[...])
        x1_ref[rows, :] = x1
        h2_ref[rows, :] = _rmsnorm_rows(x1, gpre_ref[...]).astype(h2_ref.dtype)


def out_proj_residual(z, w_o, x, g_post, g_next_pre):
    m, d = x.shape
    tm = _tile(m, 512)
    row = lambda: pl.BlockSpec((tm, d), lambda i: (i, 0))
    vec = lambda: pl.BlockSpec((1, d), lambda i: (0, 0))
    return pl.pallas_call(
        _out_proj_kernel,
        grid=(m // tm,),
        in_specs=[row(), pl.BlockSpec((d, d), lambda i: (0, 0)), row(), vec(), vec()],
        out_specs=[row(), row()],
        out_shape=[jax.ShapeDtypeStruct((m, d), F32), jax.ShapeDtypeStruct((m, d), F32)],
        compiler_params=_params(1),
        name="out_proj_residual",
    )(z, w_o, x, g_post.reshape(1, d), g_next_pre.reshape(1, d))


def _ffn_up_kernel(h_ref, wg_ref, wv_ref, cwg_ref, cwv_ref, o_ref, carry_g, carry_v, *, tiles_per_seq):
    i = pl.program_id(0)
    j = pl.program_id(1)
    start = i % tiles_per_seq == 0
    prev_g = _load_carry(carry_g, j, start)
    prev_v = _load_carry(carry_v, j, start)
    for r0 in range(0, h_ref.shape[0], FFN_UP_ROWS):
        h = h_ref[r0:r0 + FFN_UP_ROWS, :]
        zg = _dot(h, wg_ref[...])
        zv = _dot(h, wv_ref[...])
        gate = _causal_conv3(zg, prev_g, cwg_ref[...])
        val = _causal_conv3(zv, prev_v, cwv_ref[...])
        o_ref[r0:r0 + FFN_UP_ROWS, :] = (gate * jax.nn.sigmoid(gate) * val).astype(o_ref.dtype)
        prev_g = zg[FFN_UP_ROWS - SUBLANES:, :]
        prev_v = zv[FFN_UP_ROWS - SUBLANES:, :]
    carry_g[j] = prev_g
    carry_v[j] = prev_v


def ffn_up(h, w_up, conv_w, seq):
    m, d = h.shape
    d_ff = w_up.shape[1] // 2
    tm = _tile(seq, 1024)
    tn = 512
    nj = d_ff // tn
    kern = functools.partial(_ffn_up_kernel, tiles_per_seq=seq // tm)
    return pl.pallas_call(
        kern,
        grid=(m // tm, nj),
        in_specs=[pl.BlockSpec((tm, d), lambda i, j: (i, 0)),
                  pl.BlockSpec((d, tn), lambda i, j: (0, j)),
                  pl.BlockSpec((d, tn), lambda i, j: (0, nj + j)),
                  pl.BlockSpec((CONV_K, tn), lambda i, j: (0, j)),
                  pl.BlockSpec((CONV_K, tn), lambda i, j: (0, nj + j))],
        out_specs=pl.BlockSpec((tm, tn), lambda i, j: (i, j)),
        out_shape=jax.ShapeDtypeStruct((m, d_ff), BF16),
        scratch_shapes=[pltpu.VMEM((nj, SUBLANES, tn), F32), pltpu.VMEM((nj, SUBLANES, tn), F32)],
        compiler_params=_params(2),
        name="ffn_up",
    )(h, w_up, w_up, conv_w, conv_w)


def _ffn_down_kernel(a_ref, w_ref, x_ref, g_ref, o_ref):
    for r0 in range(0, a_ref.shape[0], EPILOGUE_ROWS):
        rows = slice(r0, r0 + EPILOGUE_ROWS)
        y = _dot(a_ref[rows, :], w_ref[...])
        o_ref[rows, :] = x_ref[rows, :] + _rmsnorm_rows(y, g_ref[...])


def ffn_down(a, w_down, x, g_post):
    m, d = x.shape
    d_ff = a.shape[1]
    tm = _tile(m, 512)
    row = lambda: pl.BlockSpec((tm, d), lambda i: (i, 0))
    return pl.pallas_call(
        _ffn_down_kernel,
        grid=(m // tm,),
        in_specs=[pl.BlockSpec((tm, d_ff), lambda i: (i, 0)),
                  pl.BlockSpec((d_ff, d), lambda i: (0, 0)),
                  row(), pl.BlockSpec((1, d), lambda i: (0, 0))],
        out_specs=row(),
        out_shape=jax.ShapeDtypeStruct((m, d), F32),
        compiler_params=_params(1),
        name="ffn_down",
    )(a, w_down, x, g_post.reshape(1, d))


def kernel(x, mem, g_mix_pre, w_in, w_gate, b_gate, conv_w, w_conv_out, lambda_q1, lambda_k1, lambda_q2, lambda_k2, g_diff_sub, w_diff_out, g_mem, w_mem_kv, w_cross_out, w_o, g_mix_post, g_ffn_pre, w_up, ffn_conv_w, w_down, g_ffn_post):
    bsz, seq, d = x.shape
    mem_len = mem.shape[1]
    depth = w_in.shape[0]
    m = bsz * seq
    q_lo = N_BRANCH * CONV_WIDTH
    qkvx_width = 2 * DIFF_QK_WIDTH + DIFF_WIDTH + CROSS_WIDTH
    qkvx_scale = jnp.where(jnp.arange(qkvx_width) < DIFF_QK_WIDTH,
                           LOG2E * DIFF_QK_DIM ** -0.5, 1.0).astype(F32)
    kv_scale = jnp.ones((2 * CROSS_WIDTH,), F32)

    xf = x.reshape(m, d)
    for l in range(depth):
        lambda_init = 0.8 - 0.6 * math.exp(-0.3 * l)
        w_in_b = w_in[l].astype(BF16)
        ya, h = conv_mixer(xf, g_mix_pre[l], w_in_b, conv_w[l], seq)
        qkvx = project(h, w_in_b, qkvx_scale, q_lo // 1024).reshape(bsz, seq, qkvx_width)

        lam_rows = jnp.stack([lambda_q1[l], lambda_k1[l], lambda_q2[l], lambda_k2[l]])
        yd = diff_attention(qkvx, lam_rows, g_diff_sub[l], bsz, seq, lambda_init)

        mem_n = rmsnorm_bf16(mem.reshape(bsz * mem_len, d), g_mem[l])
        kv = project(mem_n, w_mem_kv[l].astype(BF16), kv_scale, 0, tm_want=bsz * mem_len)
        yx = cross_attention(qkvx, kv.reshape(bsz, mem_len, 2 * CROSS_WIDTH), bsz, seq)

        merged = gated_merge(h, ya, yd.reshape(m, DIFF_WIDTH), yx.reshape(m, CROSS_WIDTH),
                             w_gate[l].astype(BF16), b_gate[l].reshape(1, -1),
                             w_conv_out[l].astype(BF16), w_diff_out[l].astype(BF16),
                             w_cross_out[l].astype(BF16))
        xf, h2 = out_proj_residual(merged, w_o[l].astype(BF16), xf, g_mix_post[l], g_ffn_pre[l])

        act = ffn_up(h2, w_up[l].astype(BF16), ffn_conv_w[l], seq)
        xf = ffn_down(act, w_down[l].astype(BF16), xf, g_ffn_post[l])
    return xf.reshape(bsz, seq, d)
```

```python
import functools
import math

import jax
import jax.numpy as jnp
from jax import lax
from jax.experimental import pallas as pl
from jax.experimental.pallas import tpu as pltpu

EPS = 1e-6
CHUNK = 64
CONV_WIDTH = 1024
CONV_K = 3
DIFF_HEADS = 8
DIFF_QK_DIM = 64
DIFF_V_DIM = 2 * DIFF_QK_DIM
DIFF_QK_WIDTH = DIFF_HEADS * 2 * DIFF_QK_DIM
DIFF_WIDTH = DIFF_HEADS * DIFF_V_DIM
CROSS_HEADS = 4
CROSS_HEAD_DIM = 256
CROSS_WIDTH = CROSS_HEADS * CROSS_HEAD_DIM
N_BRANCH = 3
LOG2E = math.log2(math.e)
MAX_EXCESS = 30.0
GROUP = 32

SUBLANES = 8
BF16_ROWS = 16
FFN_UP_ROWS = 128
EPILOGUE_ROWS = 256
VMEM_LIMIT = 56 * 1024 * 1024

F32 = jnp.float32
BF16 = jnp.bfloat16


def _params(n_axes):
    return pltpu.CompilerParams(
        dimension_semantics=("arbitrary",) * n_axes, vmem_limit_bytes=VMEM_LIMIT)


def _dot(a, b):
    return lax.dot_general(a, b, (((1,), (0,)), ((), ())), preferred_element_type=F32)


def _dot_nt(a, b):
    return lax.dot_general(a, b, (((1,), (1,)), ((), ())), preferred_element_type=F32)


def _tile(n, want):
    t = min(n, want)
    assert n % t == 0, (n, want)
    return t


def _rmsnorm_rows(x, g):
    ms = jnp.mean(x * x, axis=-1, keepdims=True)
    return x * lax.rsqrt(ms + EPS) * g


def _rmsnorm_kernel(x_ref, g_ref, o_ref):
    o_ref[...] = _rmsnorm_rows(x_ref[...], g_ref[...]).astype(o_ref.dtype)


def rmsnorm_bf16(x, g):
    m, d = x.shape
    tm = _tile(m, 512)
    return pl.pallas_call(
        _rmsnorm_kernel,
        grid=(m // tm,),
        in_specs=[pl.BlockSpec((tm, d), lambda i: (i, 0)),
                  pl.BlockSpec((1, d), lambda i: (0, 0))],
        out_specs=pl.BlockSpec((tm, d), lambda i: (i, 0)),
        out_shape=jax.ShapeDtypeStruct((m, d), BF16),
        compiler_params=_params(1),
        name="rmsnorm",
    )(x, g.reshape(1, d))


def _shift_rows(u, prev, s):
    rolled = pltpu.roll(u, s, axis=0)
    rprev = pltpu.roll(prev, s, axis=0)
    rows = lax.broadcasted_iota(jnp.int32, prev.shape, 0)
    first = jnp.where(rows < s, rprev, rolled[0:SUBLANES])
    return jnp.concatenate([first, rolled[SUBLANES:]], axis=0)


def _causal_conv3(u, prev, w):
    return (w[2:3] * u + w[1:2] * _shift_rows(u, prev, 1) + w[0:1] * _shift_rows(u, prev, 2))


def _load_carry(carry_ref, j, at_seq_start):
    @pl.when(at_seq_start)
    def _():
        carry_ref[j] = jnp.zeros(carry_ref.shape[1:], carry_ref.dtype)
    return carry_ref[j]


def _conv_mixer_kernel(x_ref, g_ref, wb_ref, wc_ref, wv_ref, cw_ref, o_ref, h_ref, carry_ref, *, tiles_per_seq):
    i = pl.program_id(0)
    j = pl.program_id(1)

    @pl.when(j == 0)
    def _():
        h_ref[...] = _rmsnorm_rows(x_ref[...], g_ref[...]).astype(h_ref.dtype)

    h = h_ref[...]
    u = _dot(h, wc_ref[...]) * _dot(h, wv_ref[...])
    prev = _load_carry(carry_ref, j, i % tiles_per_seq == 0)
    conv = _causal_conv3(u, prev, cw_ref[...])
    carry_ref[j] = u[u.shape[0] - SUBLANES:, :]
    o_ref[...] = (_dot(h, wb_ref[...]) * conv).astype(o_ref.dtype)


def conv_mixer(x, g_pre, w_in, conv_w, seq):
    m, d = x.shape
    tm = _tile(seq, 1024)
    tn = 512
    nj = CONV_WIDTH // tn
    kern = functools.partial(_conv_mixer_kernel, tiles_per_seq=seq // tm)
    return pl.pallas_call(
        kern,
        grid=(m // tm, nj),
        in_specs=[pl.BlockSpec((tm, d), lambda i, j: (i, 0)),
                  pl.BlockSpec((1, d), lambda i, j: (0, 0)),
                  pl.BlockSpec((d, tn), lambda i, j: (0, j)),
                  pl.BlockSpec((d, tn), lambda i, j: (0, nj + j)),
                  pl.BlockSpec((d, tn), lambda i, j: (0, 2 * nj + j)),
                  pl.BlockSpec((CONV_K, tn), lambda i, j: (0, j))],
        out_specs=[pl.BlockSpec((tm, tn), lambda i, j: (i, j)),
                   pl.BlockSpec((tm, d), lambda i, j: (i, 0))],
        out_shape=[jax.ShapeDtypeStruct((m, CONV_WIDTH), BF16), jax.ShapeDtypeStruct((m, d), BF16)],
        scratch_shapes=[pltpu.VMEM((nj, SUBLANES, tn), F32)],
        compiler_params=_params(2),
        name="conv_mixer",
    )(x, g_pre.reshape(1, d), w_in, w_in, w_in, conv_w)


def _proj_kernel(h_ref, w_ref, scale_ref, o_ref):
    o_ref[...] = (_dot(h_ref[...], w_ref[...]) * scale_ref[...]).astype(o_ref.dtype)


def project(h, w, col_scale, col_block0, tm_want=1024, tn=1024):
    m, d = h.shape
    n_out = col_scale.shape[0]
    tm = _tile(m, tm_want)
    return pl.pallas_call(
        _proj_kernel,
        grid=(m // tm, n_out // tn),
        in_specs=[pl.BlockSpec((tm, d), lambda i, j: (i, 0)),
                  pl.BlockSpec((d, tn), lambda i, j: (0, col_block0 + j)),
                  pl.BlockSpec((1, tn), lambda i, j: (0, j))],
        out_specs=pl.BlockSpec((tm, tn), lambda i, j: (i, j)),
        out_shape=jax.ShapeDtypeStruct((m, n_out), BF16),
        compiler_params=_params(2),
        name="project",
    )(h, w, col_scale.reshape(1, n_out))


def _col_max(s):
    r = s.shape[0]
    while r > SUBLANES:
        r //= 2
        s = jnp.maximum(s[:r], s[r:])
    return jnp.max(s, axis=0, keepdims=True)


def _diff_attn_kernel(lam_ref, q_ref, k_ref, v_ref, gsub_ref, o_ref,
                      vt_ref, s_ref, p_ref, alpha_ref, m_ref, excess_ref, acc_ref, *, tq, tk, lambda_init):
    qi = pl.program_id(2)
    seq = k_ref.shape[1]
    dv = DIFF_V_DIM
    assert tq == 2 * tk

    @pl.when(qi == 0)
    def _():
        vt_ref[dv:, :] = jnp.ones((BF16_ROWS, seq), BF16)

        def body(c, carry):
            r = pl.multiple_of(c * tq, tq)
            vt_ref[0:dv, pl.ds(r, tq)] = v_ref[0, pl.ds(r, tq), :].astype(F32).T.astype(BF16)
            return carry
        lax.fori_loop(0, seq // tq, body, 0)

    q = q_ref[0].astype(F32)
    lane = lax.broadcasted_iota(jnp.int32, q.shape, 1)
    qz = (jnp.where(lane < DIFF_QK_DIM, q, 0.0).astype(BF16),
          jnp.where(lane >= DIFF_QK_DIM, q, 0.0).astype(BF16))

    def k_tile(t):
        return k_ref[0, pl.ds(pl.multiple_of(t * tk, tk), tk), :]

    def vt_tile(t):
        return vt_ref[:, pl.ds(pl.multiple_of(t * tk, tk), tk)]

    key_chunk = lax.broadcasted_iota(jnp.int32, (tk, tq), 0) // CHUNK
    qry_chunk = lax.broadcasted_iota(jnp.int32, (tk, tq), 1) // CHUNK
    diag_masks = (key_chunk <= qry_chunk, key_chunk + tk // CHUNK <= qry_chunk)

    def exact_tile(t):
        k_t = k_tile(t)
        vt_t = vt_tile(t)
        for c in range(2):
            s = _dot_nt(k_t, qz[c])
            m_old = m_ref[c]
            m_new = jnp.maximum(m_old, _col_max(s))
            p = jnp.exp2(s - m_new).astype(BF16)
            acc_ref[c] = jnp.exp2(m_old - m_new) * acc_ref[c] + _dot(vt_t, p)
            m_ref[c] = m_new

    def exact_diagonal():
        k_d = [k_tile(2 * qi + d) for d in range(2)]
        vt_d = [vt_tile(2 * qi + d) for d in range(2)]
        for c in range(2):
            s = [jnp.where(diag_masks[d], _dot_nt(k_d[d], qz[c]), -jnp.inf) for d in range(2)]
            m = jnp.maximum(_col_max(s[0]), _col_max(s[1]))
            acc_ref[c] = (_dot(vt_d[0], jnp.exp2(s[0] - m).astype(BF16))
                          + _dot(vt_d[1], jnp.exp2(s[1] - m).astype(BF16)))
            m_ref[c] = m

    def fast_group(t0, n, masks=None):
        assert 2 <= n <= GROUP
        slots = range(GROUP - n, GROUP)
        m_frozen = [m_ref[c] for c in range(2)]
        tile_max = [None, None]
        if masks is None:
            k_next = k_tile(t0 + n)
            s_next = [_dot_nt(k_next, qz[c]) for c in range(2)]
        for c in range(2):
            s = s_ref[c]
            if masks is not None:
                s = jnp.where(masks[0], s, -jnp.inf)
            tile_max[c] = _col_max(s)
            p_ref[slots[0], c] = jnp.exp2(s - m_frozen[c]).astype(BF16)
            if masks is None:
                s_ref[c] = s_next[c]
        vt_pending = vt_tile(jnp.maximum(t0 - 1, 0))
        acc = [alpha_ref[c] * (acc_ref[c] + _dot(vt_pending, p_ref[GROUP - 1, c])) for c in range(2)]
        for i in range(1, n):
            k_t = k_tile(t0 + i)
            for c in range(2):
                s = _dot_nt(k_t, qz[c])
                if masks is not None:
                    s = jnp.where(masks[i], s, -jnp.inf)
                tile_max[c] = jnp.maximum(tile_max[c], _col_max(s))
                p_ref[slots[i], c] = jnp.exp2(s - m_frozen[c]).astype(BF16)
            vt_t = vt_tile(t0 + i - 1)
            for c in range(2):
                acc[c] = acc[c] + _dot(vt_t, p_ref[slots[i - 1], c])
        if masks is not None:
            vt_t = vt_tile(t0 + n - 1)
            for c in range(2):
                acc[c] = acc[c] + _dot(vt_t, p_ref[slots[n - 1], c])
        for c in range(2):
            acc_ref[c] = acc[c]
            m_new = jnp.maximum(m_frozen[c], tile_max[c])
            alpha_ref[c] = jnp.exp2(m_frozen[c] - m_new)
            excess_ref[c] = jnp.maximum(excess_ref[c], tile_max[c] - m_frozen[c])
            m_ref[c] = m_new

    for c in range(2):
        s0 = _dot_nt(k_tile(0), qz[c])
        s_ref[c] = s0
        m_ref[c] = _col_max(s0[0:CHUNK])
    acc_ref[...] = jnp.zeros(acc_ref.shape, F32)
    p_ref[GROUP - 1] = jnp.zeros(p_ref.shape[1:], BF16)
    alpha_ref[...] = jnp.ones(alpha_ref.shape, F32)
    excess_ref[...] = jnp.zeros(excess_ref.shape, F32)
    n_tiles = 2 * qi
    rem = n_tiles % GROUP
    size = 2
    while size < GROUP:
        @pl.when(rem & size > 0)
        def _(size=size):
            fast_group(rem & (size - 1), size)
        size *= 2

    def group(j, carry):
        fast_group(rem + GROUP * j, GROUP)
        return carry
    lax.fori_loop(0, n_tiles // GROUP, group, 0)
    fast_group(n_tiles, 2, diag_masks)

    worst = jnp.max(jnp.maximum(excess_ref[0], excess_ref[1]), axis=-1, keepdims=True)

    @pl.when(worst[0, 0] > MAX_EXCESS)
    def _():
        exact_diagonal()

        def body(t, carry):
            exact_tile(t)
            return carry
        lax.fori_loop(0, 2 * qi, body, 0)

    lv = lam_ref[...]
    lam = (jnp.exp(jnp.sum(lv[0:1] * lv[1:2], axis=-1, keepdims=True))
           - jnp.exp(jnp.sum(lv[2:3] * lv[3:4], axis=-1, keepdims=True)) + lambda_init)
    a1 = acc_ref[0]
    a2 = acc_ref[1]
    o = a1[0:dv] / a1[dv:dv + 1] - lam * (a2[0:dv] / a2[dv:dv + 1])
    ms = jnp.mean(o * o, axis=0, keepdims=True)
    y = o * lax.rsqrt(ms + EPS) * gsub_ref[...]
    y = y * (1.0 - lambda_init)
    o_ref[0] = y.T.astype(o_ref.dtype)


def diff_attention(qkvx, lam_rows, g_sub, bsz, seq, lambda_init):
    tq = _tile(seq, 512)
    tk = tq // 2
    nq = seq // tq
    dv = DIFF_V_DIM
    kern = functools.partial(_diff_attn_kernel, tq=tq, tk=tk, lambda_init=lambda_init)
    return pl.pallas_call(
        kern,
        grid=(bsz, DIFF_HEADS, nq),
        in_specs=[pl.BlockSpec((4, DIFF_QK_DIM), lambda b, h, i: (0, 0)),
                  pl.BlockSpec((1, tq, dv), lambda b, h, i: (b, i, h)),
                  pl.BlockSpec((1, seq, dv), lambda b, h, i: (b, 0, DIFF_HEADS + h)),
                  pl.BlockSpec((1, seq, dv), lambda b, h, i: (b, 0, 2 * DIFF_HEADS + h)),
                  pl.BlockSpec((dv, 1), lambda b, h, i: (0, 0))],
        out_specs=pl.BlockSpec((1, tq, dv), lambda b, h, i: (b, i, h)),
        out_shape=jax.ShapeDtypeStruct((bsz, seq, DIFF_WIDTH), BF16),
        scratch_shapes=[pltpu.VMEM((dv + BF16_ROWS, seq), BF16),
                        pltpu.VMEM((2, tk, tq), F32),
                        pltpu.VMEM((GROUP, 2, tk, tq), BF16),
                        pltpu.VMEM((2, 1, tq), F32),
                        pltpu.VMEM((2, 1, tq), F32),
                        pltpu.VMEM((2, 1, tq), F32),
                        pltpu.VMEM((2, dv + BF16_ROWS, tq), F32)],
        compiler_params=_params(3),
        name="diff_attention",
    )(lam_rows, qkvx, qkvx, qkvx, g_sub.reshape(dv, 1))


def _cross_attn_kernel(xq_ref, kv_ref, o_ref):
    hd = CROSS_HEAD_DIM
    for h in range(CROSS_HEADS):
        q = xq_ref[0, :, h * hd:(h + 1) * hd]
        k = kv_ref[0, :, h * hd:(h + 1) * hd]
        v = kv_ref[0, :, CROSS_WIDTH + h * hd:CROSS_WIDTH + (h + 1) * hd]
        s = _dot_nt(q, k) * (hd ** -0.5)
        p = jnp.exp(s - jnp.max(s, axis=-1, keepdims=True))
        l = jnp.sum(p, axis=-1, keepdims=True)
        o_ref[0, :, h * hd:(h + 1) * hd] = (_dot(p.astype(BF16), v) / l).astype(o_ref.dtype)


def cross_attention(qkvx, kv, bsz, seq):
    tm = _tile(seq, 512)
    mem_len = kv.shape[1]
    return pl.pallas_call(
        _cross_attn_kernel,
        grid=(bsz, seq // tm),
        in_specs=[pl.BlockSpec((1, tm, CROSS_WIDTH), lambda b, i: (b, i, 3)),
                  pl.BlockSpec((1, mem_len, 2 * CROSS_WIDTH), lambda b, i: (b, 0, 0))],
        out_specs=pl.BlockSpec((1, tm, CROSS_WIDTH), lambda b, i: (b, i, 0)),
        out_shape=jax.ShapeDtypeStruct((bsz, seq, CROSS_WIDTH), BF16),
        compiler_params=_params(2),
        name="cross_attention",
    )(qkvx, kv)


def _merge_kernel(h_ref, ya_ref, yd_ref, yx_ref, wga_ref, wgb_ref, wgc_ref,
                  ba_ref, bb_ref, bc_ref, wa_ref, wd_ref, wx_ref, o_ref):
    h = h_ref[...]
    merged = jax.nn.sigmoid(_dot(h, wga_ref[...]) + ba_ref[...]) * _dot(ya_ref[...], wa_ref[...])
    merged += jax.nn.sigmoid(_dot(h, wgb_ref[...]) + bb_ref[...]) * _dot(yd_ref[...], wd_ref[...])
    merged += jax.nn.sigmoid(_dot(h, wgc_ref[...]) + bc_ref[...]) * _dot(yx_ref[...], wx_ref[...])
    o_ref[...] = merged.astype(o_ref.dtype)


def gated_merge(h, ya, yd, yx, w_gate, b_gate, w_a, w_d, w_x):
    m, d = h.shape
    tm = _tile(m, 1024)
    tn = 512
    nj = d // tn
    act = lambda width: pl.BlockSpec((tm, width), lambda i, j: (i, 0))
    gate_w = lambda br: pl.BlockSpec((d, tn), lambda i, j: (0, br * nj + j))
    gate_b = lambda br: pl.BlockSpec((1, tn), lambda i, j: (0, br * nj + j))
    out_w = lambda width: pl.BlockSpec((width, tn), lambda i, j: (0, j))
    return pl.pallas_call(
        _merge_kernel,
        grid=(m // tm, nj),
        in_specs=[act(d), act(CONV_WIDTH), act(DIFF_WIDTH), act(CROSS_WIDTH),
                  gate_w(0), gate_w(1), gate_w(2), gate_b(0), gate_b(1), gate_b(2),
                  out_w(CONV_WIDTH), out_w(DIFF_WIDTH), out_w(CROSS_WIDTH)],
        out_specs=pl.BlockSpec((tm, tn), lambda i, j: (i, j)),
        out_shape=jax.ShapeDtypeStruct((m, d), BF16),
        compiler_params=_params(2),
        name="gated_merge",
    )(h, ya, yd, yx, w_gate, w_gate, w_gate, b_gate, b_gate, b_gate, w_a, w_d, w_x)


def _out_proj_kernel(z_ref, w_ref, x_ref, gpost_ref, gpre_ref, x1_ref, h2_ref):
    for r0 in range(0, z_ref.shape[0], EPILOGUE_ROWS):
        rows = slice(r0, r0 + EPILOGUE_ROWS)
        y = _dot(z_ref[rows, :], w_ref[...])
        x1 = x_ref[rows, :] + _rmsnorm_rows(y, gpost_ref[...])
        x1_ref[rows, :] = x1
        h2_ref[rows, :] = _rmsnorm_rows(x1, gpre_ref[...]).astype(h2_ref.dtype)


def out_proj_residual(z, w_o, x, g_post, g_next_pre):
    m, d = x.shape
    tm = _tile(m, 512)
    row = lambda: pl.BlockSpec((tm, d), lambda i: (i, 0))
    vec = lambda: pl.BlockSpec((1, d), lambda i: (0, 0))
    return pl.pallas_call(
        _out_proj_kernel,
        grid=(m // tm,),
        in_specs=[row(), pl.BlockSpec((d, d), lambda i: (0, 0)), row(), vec(), vec()],
        out_specs=[row(), row()],
        out_shape=[jax.ShapeDtypeStruct((m, d), F32), jax.ShapeDtypeStruct((m, d), F32)],
        compiler_params=_params(1),
        name="out_proj_residual",
    )(z, w_o, x, g_post.reshape(1, d), g_next_pre.reshape(1, d))


def _ffn_up_kernel(h_ref, wg_ref, wv_ref, cwg_ref, cwv_ref, o_ref, carry_g, carry_v, *, tiles_per_seq):
    i = pl.program_id(0)
    j = pl.program_id(1)
    start = i % tiles_per_seq == 0
    prev_g = _load_carry(carry_g, j, start)
    prev_v = _load_carry(carry_v, j, start)
    for r0 in range(0, h_ref.shape[0], FFN_UP_ROWS):
        h = h_ref[r0:r0 + FFN_UP_ROWS, :]
        zg = _dot(h, wg_ref[...])
        zv = _dot(h, wv_ref[...])
        gate = _causal_conv3(zg, prev_g, cwg_ref[...])
        val = _causal_conv3(zv, prev_v, cwv_ref[...])
        o_ref[r0:r0 + FFN_UP_ROWS, :] = (gate * jax.nn.sigmoid(gate) * val).astype(o_ref.dtype)
        prev_g = zg[FFN_UP_ROWS - SUBLANES:, :]
        prev_v = zv[FFN_UP_ROWS - SUBLANES:, :]
    carry_g[j] = prev_g
    carry_v[j] = prev_v


def ffn_up(h, w_up, conv_w, seq):
    m, d = h.shape
    d_ff = w_up.shape[1] // 2
    tm = _tile(seq, 2048)
    tn = 512
    nj = d_ff // tn
    kern = functools.partial(_ffn_up_kernel, tiles_per_seq=seq // tm)
    return pl.pallas_call(
        kern,
        grid=(m // tm, nj),
        in_specs=[pl.BlockSpec((tm, d), lambda i, j: (i, 0)),
                  pl.BlockSpec((d, tn), lambda i, j: (0, j)),
                  pl.BlockSpec((d, tn), lambda i, j: (0, nj + j)),
                  pl.BlockSpec((CONV_K, tn), lambda i, j: (0, j)),
                  pl.BlockSpec((CONV_K, tn), lambda i, j: (0, nj + j))],
        out_specs=pl.BlockSpec((tm, tn), lambda i, j: (i, j)),
        out_shape=jax.ShapeDtypeStruct((m, d_ff), BF16),
        scratch_shapes=[pltpu.VMEM((nj, SUBLANES, tn), F32), pltpu.VMEM((nj, SUBLANES, tn), F32)],
        compiler_params=_params(2),
        name="ffn_up",
    )(h, w_up, w_up, conv_w, conv_w)


def _ffn_down_kernel(a_ref, w_ref, x_ref, g_ref, o_ref):
    for r0 in range(0, a_ref.shape[0], EPILOGUE_ROWS):
        rows = slice(r0, r0 + EPILOGUE_ROWS)
        y = _dot(a_ref[rows, :], w_ref[...])
        o_ref[rows, :] = x_ref[rows, :] + _rmsnorm_rows(y, g_ref[...])


def ffn_down(a, w_down, x, g_post):
    m, d = x.shape
    d_ff = a.shape[1]
    tm = _tile(m, 512)
    row = lambda: pl.BlockSpec((tm, d), lambda i: (i, 0))
    return pl.pallas_call(
        _ffn_down_kernel,
        grid=(m // tm,),
        in_specs=[pl.BlockSpec((tm, d_ff), lambda i: (i, 0)),
                  pl.BlockSpec((d_ff, d), lambda i: (0, 0)),
                  row(), pl.BlockSpec((1, d), lambda i: (0, 0))],
        out_specs=row(),
        out_shape=jax.ShapeDtypeStruct((m, d), F32),
        compiler_params=_params(1),
        name="ffn_down",
    )(a, w_down, x, g_post.reshape(1, d))


def kernel(x, mem, g_mix_pre, w_in, w_gate, b_gate, conv_w, w_conv_out, lambda_q1, lambda_k1, lambda_q2, lambda_k2, g_diff_sub, w_diff_out, g_mem, w_mem_kv, w_cross_out, w_o, g_mix_post, g_ffn_pre, w_up, ffn_conv_w, w_down, g_ffn_post):
    bsz, seq, d = x.shape
    mem_len = mem.shape[1]
    depth = w_in.shape[0]
    m = bsz * seq
    q_lo = N_BRANCH * CONV_WIDTH
    qkvx_width = 2 * DIFF_QK_WIDTH + DIFF_WIDTH + CROSS_WIDTH
    qkvx_scale = jnp.where(jnp.arange(qkvx_width) < DIFF_QK_WIDTH,
                           LOG2E * DIFF_QK_DIM ** -0.5, 1.0).astype(F32)
    kv_scale = jnp.ones((2 * CROSS_WIDTH,), F32)

    xf = x.reshape(m, d)
    for l in range(depth):
        lambda_init = 0.8 - 0.6 * math.exp(-0.3 * l)
        w_in_b = w_in[l].astype(BF16)
        ya, h = conv_mixer(xf, g_mix_pre[l], w_in_b, conv_w[l], seq)
        qkvx = project(h, w_in_b, qkvx_scale, q_lo // 1024).reshape(bsz, seq, qkvx_width)

        lam_rows = jnp.stack([lambda_q1[l], lambda_k1[l], lambda_q2[l], lambda_k2[l]])
        yd = diff_attention(qkvx, lam_rows, g_diff_sub[l], bsz, seq, lambda_init)

        mem_n = rmsnorm_bf16(mem.reshape(bsz * mem_len, d), g_mem[l])
        kv = project(mem_n, w_mem_kv[l].astype(BF16), kv_scale, 0, tm_want=bsz * mem_len)
        yx = cross_attention(qkvx, kv.reshape(bsz, mem_len, 2 * CROSS_WIDTH), bsz, seq)

        merged = gated_merge(h, ya, yd.reshape(m, DIFF_WIDTH), yx.reshape(m, CROSS_WIDTH),
                             w_gate[l].astype(BF16), b_gate[l].reshape(1, -1),
                             w_conv_out[l].astype(BF16), w_diff_out[l].astype(BF16),
                             w_cross_out[l].astype(BF16))
        xf, h2 = out_proj_residual(merged, w_o[l].astype(BF16), xf, g_mix_post[l], g_ffn_pre[l])

        act = ffn_up(h2, w_up[l].astype(BF16), ffn_conv_w[l], seq)
        xf = ffn_down(act, w_down[l].astype(BF16), xf, g_ffn_post[l])
    return xf.reshape(bsz, seq, d)
```

```python
import functools
import math

import jax
import jax.numpy as jnp
from jax import lax
from jax.experimental import pallas as pl
from jax.experimental.pallas import tpu as pltpu

EPS = 1e-6
CHUNK = 64
CONV_WIDTH = 1024
CONV_K = 3
DIFF_HEADS = 8
DIFF_QK_DIM = 64
DIFF_V_DIM = 2 * DIFF_QK_DIM
DIFF_QK_WIDTH = DIFF_HEADS * 2 * DIFF_QK_DIM
DIFF_WIDTH = DIFF_HEADS * DIFF_V_DIM
CROSS_HEADS = 4
CROSS_HEAD_DIM = 256
CROSS_WIDTH = CROSS_HEADS * CROSS_HEAD_DIM
N_BRANCH = 3
LOG2E = math.log2(math.e)
MAX_EXCESS = 30.0
GROUP = 32

SUBLANES = 8
BF16_ROWS = 16
FFN_UP_ROWS = 128
EPILOGUE_ROWS = 256
VMEM_LIMIT = 56 * 1024 * 1024

F32 = jnp.float32
BF16 = jnp.bfloat16


def _params(n_axes):
    return pltpu.CompilerParams(
        dimension_semantics=("arbitrary",) * n_axes, vmem_limit_bytes=VMEM_LIMIT)


def _dot(a, b):
    return lax.dot_general(a, b, (((1,), (0,)), ((), ())), preferred_element_type=F32)


def _dot_nt(a, b):
    return lax.dot_general(a, b, (((1,), (1,)), ((), ())), preferred_element_type=F32)


def _tile(n, want):
    t = min(n, want)
    assert n % t == 0, (n, want)
    return t


def _rmsnorm_rows(x, g):
    ms = jnp.mean(x * x, axis=-1, keepdims=True)
    return x * lax.rsqrt(ms + EPS) * g


def _rmsnorm_kernel(x_ref, g_ref, o_ref):
    o_ref[...] = _rmsnorm_rows(x_ref[...], g_ref[...]).astype(o_ref.dtype)


def rmsnorm_bf16(x, g):
    m, d = x.shape
    tm = _tile(m, 512)
    return pl.pallas_call(
        _rmsnorm_kernel,
        grid=(m // tm,),
        in_specs=[pl.BlockSpec((tm, d), lambda i: (i, 0)),
                  pl.BlockSpec((1, d), lambda i: (0, 0))],
        out_specs=pl.BlockSpec((tm, d), lambda i: (i, 0)),
        out_shape=jax.ShapeDtypeStruct((m, d), BF16),
        compiler_params=_params(1),
        name="rmsnorm",
    )(x, g.reshape(1, d))


def _shift_rows(u, prev, s):
    rolled = pltpu.roll(u, s, axis=0)
    rprev = pltpu.roll(prev, s, axis=0)
    rows = lax.broadcasted_iota(jnp.int32, prev.shape, 0)
    first = jnp.where(rows < s, rprev, rolled[0:SUBLANES])
    return jnp.concatenate([first, rolled[SUBLANES:]], axis=0)


def _causal_conv3(u, prev, w):
    return (w[2:3] * u + w[1:2] * _shift_rows(u, prev, 1) + w[0:1] * _shift_rows(u, prev, 2))


def _load_carry(carry_ref, j, at_seq_start):
    @pl.when(at_seq_start)
    def _():
        carry_ref[j] = jnp.zeros(carry_ref.shape[1:], carry_ref.dtype)
    return carry_ref[j]


def _conv_mixer_kernel(x_ref, g_ref, wb_ref, wc_ref, wv_ref, cw_ref, o_ref, h_ref, carry_ref, *, tiles_per_seq):
    i = pl.program_id(0)
    j = pl.program_id(1)

    @pl.when(j == 0)
    def _():
        h_ref[...] = _rmsnorm_rows(x_ref[...], g_ref[...]).astype(h_ref.dtype)

    h = h_ref[...]
    u = _dot(h, wc_ref[...]) * _dot(h, wv_ref[...])
    prev = _load_carry(carry_ref, j, i % tiles_per_seq == 0)
    conv = _causal_conv3(u, prev, cw_ref[...])
    carry_ref[j] = u[u.shape[0] - SUBLANES:, :]
    o_ref[...] = (_dot(h, wb_ref[...]) * conv).astype(o_ref.dtype)


def conv_mixer(x, g_pre, w_in, conv_w, seq):
    m, d = x.shape
    tm = _tile(seq, 1024)
    tn = 512
    nj = CONV_WIDTH // tn
    kern = functools.partial(_conv_mixer_kernel, tiles_per_seq=seq // tm)
    return pl.pallas_call(
        kern,
        grid=(m // tm, nj),
        in_specs=[pl.BlockSpec((tm, d), lambda i, j: (i, 0)),
                  pl.BlockSpec((1, d), lambda i, j: (0, 0)),
                  pl.BlockSpec((d, tn), lambda i, j: (0, j)),
                  pl.BlockSpec((d, tn), lambda i, j: (0, nj + j)),
                  pl.BlockSpec((d, tn), lambda i, j: (0, 2 * nj + j)),
                  pl.BlockSpec((CONV_K, tn), lambda i, j: (0, j))],
        out_specs=[pl.BlockSpec((tm, tn), lambda i, j: (i, j)),
                   pl.BlockSpec((tm, d), lambda i, j: (i, 0))],
        out_shape=[jax.ShapeDtypeStruct((m, CONV_WIDTH), BF16), jax.ShapeDtypeStruct((m, d), BF16)],
        scratch_shapes=[pltpu.VMEM((nj, SUBLANES, tn), F32)],
        compiler_params=_params(2),
        name="conv_mixer",
    )(x, g_pre.reshape(1, d), w_in, w_in, w_in, conv_w)


def _proj_kernel(h_ref, w_ref, scale_ref, o_ref):
    o_ref[...] = (_dot(h_ref[...], w_ref[...]) * scale_ref[...]).astype(o_ref.dtype)


def project(h, w, col_scale, col_block0, tm_want=2048, tn=1024):
    m, d = h.shape
    n_out = col_scale.shape[0]
    tm = _tile(m, tm_want)
    return pl.pallas_call(
        _proj_kernel,
        grid=(m // tm, n_out // tn),
        in_specs=[pl.BlockSpec((tm, d), lambda i, j: (i, 0)),
                  pl.BlockSpec((d, tn), lambda i, j: (0, col_block0 + j)),
                  pl.BlockSpec((1, tn), lambda i, j: (0, j))],
        out_specs=pl.BlockSpec((tm, tn), lambda i, j: (i, j)),
        out_shape=jax.ShapeDtypeStruct((m, n_out), BF16),
        compiler_params=_params(2),
        name="project",
    )(h, w, col_scale.reshape(1, n_out))


def _col_max(s):
    r = s.shape[0]
    while r > SUBLANES:
        r //= 2
        s = jnp.maximum(s[:r], s[r:])
    return jnp.max(s, axis=0, keepdims=True)


def _diff_attn_kernel(lam_ref, q_ref, k_ref, v_ref, gsub_ref, o_ref,
                      vt_ref, s_ref, p_ref, alpha_ref, m_ref, excess_ref, acc_ref, *, tq, tk, lambda_init):
    qi = pl.program_id(2)
    seq = k_ref.shape[1]
    dv = DIFF_V_DIM
    assert tq == 2 * tk

    @pl.when(qi == 0)
    def _():
        vt_ref[dv:, :] = jnp.ones((BF16_ROWS, seq), BF16)

        def body(c, carry):
            r = pl.multiple_of(c * tq, tq)
            vt_ref[0:dv, pl.ds(r, tq)] = v_ref[0, pl.ds(r, tq), :].astype(F32).T.astype(BF16)
            return carry
        lax.fori_loop(0, seq // tq, body, 0)

    q = q_ref[0].astype(F32)
    lane = lax.broadcasted_iota(jnp.int32, q.shape, 1)
    qz = (jnp.where(lane < DIFF_QK_DIM, q, 0.0).astype(BF16),
          jnp.where(lane >= DIFF_QK_DIM, q, 0.0).astype(BF16))

    def k_tile(t):
        return k_ref[0, pl.ds(pl.multiple_of(t * tk, tk), tk), :]

    def vt_tile(t):
        return vt_ref[:, pl.ds(pl.multiple_of(t * tk, tk), tk)]

    key_chunk = lax.broadcasted_iota(jnp.int32, (tk, tq), 0) // CHUNK
    qry_chunk = lax.broadcasted_iota(jnp.int32, (tk, tq), 1) // CHUNK
    diag_masks = (key_chunk <= qry_chunk, key_chunk + tk // CHUNK <= qry_chunk)

    def exact_tile(t):
        k_t = k_tile(t)
        vt_t = vt_tile(t)
        for c in range(2):
            s = _dot_nt(k_t, qz[c])
            m_old = m_ref[c]
            m_new = jnp.maximum(m_old, _col_max(s))
            p = jnp.exp2(s - m_new).astype(BF16)
            acc_ref[c] = jnp.exp2(m_old - m_new) * acc_ref[c] + _dot(vt_t, p)
            m_ref[c] = m_new

    def exact_diagonal():
        k_d = [k_tile(2 * qi + d) for d in range(2)]
        vt_d = [vt_tile(2 * qi + d) for d in range(2)]
        for c in range(2):
            s = [jnp.where(diag_masks[d], _dot_nt(k_d[d], qz[c]), -jnp.inf) for d in range(2)]
            m = jnp.maximum(_col_max(s[0]), _col_max(s[1]))
            acc_ref[c] = (_dot(vt_d[0], jnp.exp2(s[0] - m).astype(BF16))
                          + _dot(vt_d[1], jnp.exp2(s[1] - m).astype(BF16)))
            m_ref[c] = m

    def fast_group(t0, n, masks=None):
        assert 2 <= n <= GROUP
        slots = range(GROUP - n, GROUP)
        m_frozen = [m_ref[c] for c in range(2)]
        tile_max = [None, None]
        if masks is None:
            k_next = k_tile(t0 + n)
            s_next = [_dot_nt(k_next, qz[c]) for c in range(2)]
        for c in range(2):
            s = s_ref[c]
            if masks is not None:
                s = jnp.where(masks[0], s, -jnp.inf)
            tile_max[c] = _col_max(s)
            p_ref[slots[0], c] = jnp.exp2(s - m_frozen[c]).astype(BF16)
            if masks is None:
                s_ref[c] = s_next[c]
        vt_pending = vt_tile(jnp.maximum(t0 - 1, 0))
        acc = [alpha_ref[c] * (acc_ref[c] + _dot(vt_pending, p_ref[GROUP - 1, c])) for c in range(2)]
        for i in range(1, n):
            k_t = k_tile(t0 + i)
            for c in range(2):
                s = _dot_nt(k_t, qz[c])
                if masks is not None:
                    s = jnp.where(masks[i], s, -jnp.inf)
                tile_max[c] = jnp.maximum(tile_max[c], _col_max(s))
                p_ref[slots[i], c] = jnp.exp2(s - m_frozen[c]).astype(BF16)
            vt_t = vt_tile(t0 + i - 1)
            for c in range(2):
                acc[c] = acc[c] + _dot(vt_t, p_ref[slots[i - 1], c])
        if masks is not None:
            vt_t = vt_tile(t0 + n - 1)
            for c in range(2):
                acc[c] = acc[c] + _dot(vt_t, p_ref[slots[n - 1], c])
        for c in range(2):
            acc_ref[c] = acc[c]
            m_new = jnp.maximum(m_frozen[c], tile_max[c])
            alpha_ref[c] = jnp.exp2(m_frozen[c] - m_new)
            excess_ref[c] = jnp.maximum(excess_ref[c], tile_max[c] - m_frozen[c])
            m_ref[c] = m_new

    for c in range(2):
        s0 = _dot_nt(k_tile(0), qz[c])
        s_ref[c] = s0
        m_ref[c] = _col_max(s0[0:CHUNK])
    acc_ref[...] = jnp.zeros(acc_ref.shape, F32)
    p_ref[GROUP - 1] = jnp.zeros(p_ref.shape[1:], BF16)
    alpha_ref[...] = jnp.ones(alpha_ref.shape, F32)
    excess_ref[...] = jnp.zeros(excess_ref.shape, F32)
    n_tiles = 2 * qi
    rem = n_tiles % GROUP
    size = 2
    while size < GROUP:
        @pl.when(rem & size > 0)
        def _(size=size):
            fast_group(rem & (size - 1), size)
        size *= 2

    def group(j, carry):
        fast_group(rem + GROUP * j, GROUP)
        return carry
    lax.fori_loop(0, n_tiles // GROUP, group, 0)
    fast_group(n_tiles, 2, diag_masks)

    worst = jnp.max(jnp.maximum(excess_ref[0], excess_ref[1]), axis=-1, keepdims=True)

    @pl.when(worst[0, 0] > MAX_EXCESS)
    def _():
        exact_diagonal()

        def body(t, carry):
            exact_tile(t)
            return carry
        lax.fori_loop(0, 2 * qi, body, 0)

    lv = lam_ref[...]
    lam = (jnp.exp(jnp.sum(lv[0:1] * lv[1:2], axis=-1, keepdims=True))
           - jnp.exp(jnp.sum(lv[2:3] * lv[3:4], axis=-1, keepdims=True)) + lambda_init)
    a1 = acc_ref[0]
    a2 = acc_ref[1]
    o = a1[0:dv] / a1[dv:dv + 1] - lam * (a2[0:dv] / a2[dv:dv + 1])
    ms = jnp.mean(o * o, axis=0, keepdims=True)
    y = o * lax.rsqrt(ms + EPS) * gsub_ref[...]
    y = y * (1.0 - lambda_init)
    o_ref[0] = y.T.astype(o_ref.dtype)


def diff_attention(qkvx, lam_rows, g_sub, bsz, seq, lambda_init):
    tq = _tile(seq, 512)
    tk = tq // 2
    nq = seq // tq
    dv = DIFF_V_DIM
    kern = functools.partial(_diff_attn_kernel, tq=tq, tk=tk, lambda_init=lambda_init)
    return pl.pallas_call(
        kern,
        grid=(bsz, DIFF_HEADS, nq),
        in_specs=[pl.BlockSpec((4, DIFF_QK_DIM), lambda b, h, i: (0, 0)),
                  pl.BlockSpec((1, tq, dv), lambda b, h, i: (b, i, h)),
                  pl.BlockSpec((1, seq, dv), lambda b, h, i: (b, 0, DIFF_HEADS + h)),
                  pl.BlockSpec((1, seq, dv), lambda b, h, i: (b, 0, 2 * DIFF_HEADS + h)),
                  pl.BlockSpec((dv, 1), lambda b, h, i: (0, 0))],
        out_specs=pl.BlockSpec((1, tq, dv), lambda b, h, i: (b, i, h)),
        out_shape=jax.ShapeDtypeStruct((bsz, seq, DIFF_WIDTH), BF16),
        scratch_shapes=[pltpu.VMEM((dv + BF16_ROWS, seq), BF16),
                        pltpu.VMEM((2, tk, tq), F32),
                        pltpu.VMEM((GROUP, 2, tk, tq), BF16),
                        pltpu.VMEM((2, 1, tq), F32),
                        pltpu.VMEM((2, 1, tq), F32),
                        pltpu.VMEM((2, 1, tq), F32),
                        pltpu.VMEM((2, dv + BF16_ROWS, tq), F32)],
        compiler_params=_params(3),
        name="diff_attention",
    )(lam_rows, qkvx, qkvx, qkvx, g_sub.reshape(dv, 1))


def _cross_attn_kernel(xq_ref, kv_ref, o_ref):
    hd = CROSS_HEAD_DIM
    for h in range(CROSS_HEADS):
        q = xq_ref[0, :, h * hd:(h + 1) * hd]
        k = kv_ref[0, :, h * hd:(h + 1) * hd]
        v = kv_ref[0, :, CROSS_WIDTH + h * hd:CROSS_WIDTH + (h + 1) * hd]
        s = _dot_nt(q, k) * (hd ** -0.5)
        p = jnp.exp(s - jnp.max(s, axis=-1, keepdims=True))
        l = jnp.sum(p, axis=-1, keepdims=True)
        o_ref[0, :, h * hd:(h + 1) * hd] = (_dot(p.astype(BF16), v) / l).astype(o_ref.dtype)


def cross_attention(qkvx, kv, bsz, seq):
    tm = _tile(seq, 512)
    mem_len = kv.shape[1]
    return pl.pallas_call(
        _cross_attn_kernel,
        grid=(bsz, seq // tm),
        in_specs=[pl.BlockSpec((1, tm, CROSS_WIDTH), lambda b, i: (b, i, 3)),
                  pl.BlockSpec((1, mem_len, 2 * CROSS_WIDTH), lambda b, i: (b, 0, 0))],
        out_specs=pl.BlockSpec((1, tm, CROSS_WIDTH), lambda b, i: (b, i, 0)),
        out_shape=jax.ShapeDtypeStruct((bsz, seq, CROSS_WIDTH), BF16),
        compiler_params=_params(2),
        name="cross_attention",
    )(qkvx, kv)


def _merge_kernel(h_ref, ya_ref, yd_ref, yx_ref, wga_ref, wgb_ref, wgc_ref,
                  ba_ref, bb_ref, bc_ref, wa_ref, wd_ref, wx_ref, o_ref):
    h = h_ref[...]
    merged = jax.nn.sigmoid(_dot(h, wga_ref[...]) + ba_ref[...]) * _dot(ya_ref[...], wa_ref[...])
    merged += jax.nn.sigmoid(_dot(h, wgb_ref[...]) + bb_ref[...]) * _dot(yd_ref[...], wd_ref[...])
    merged += jax.nn.sigmoid(_dot(h, wgc_ref[...]) + bc_ref[...]) * _dot(yx_ref[...], wx_ref[...])
    o_ref[...] = merged.astype(o_ref.dtype)


def gated_merge(h, ya, yd, yx, w_gate, b_gate, w_a, w_d, w_x):
    m, d = h.shape
    tm = _tile(m, 1024)
    tn = 512
    nj = d // tn
    act = lambda width: pl.BlockSpec((tm, width), lambda i, j: (i, 0))
    gate_w = lambda br: pl.BlockSpec((d, tn), lambda i, j: (0, br * nj + j))
    gate_b = lambda br: pl.BlockSpec((1, tn), lambda i, j: (0, br * nj + j))
    out_w = lambda width: pl.BlockSpec((width, tn), lambda i, j: (0, j))
    return pl.pallas_call(
        _merge_kernel,
        grid=(m // tm, nj),
        in_specs=[act(d), act(CONV_WIDTH), act(DIFF_WIDTH), act(CROSS_WIDTH),
                  gate_w(0), gate_w(1), gate_w(2), gate_b(0), gate_b(1), gate_b(2),
                  out_w(CONV_WIDTH), out_w(DIFF_WIDTH), out_w(CROSS_WIDTH)],
        out_specs=pl.BlockSpec((tm, tn), lambda i, j: (i, j)),
        out_shape=jax.ShapeDtypeStruct((m, d), BF16),
        compiler_params=_params(2),
        name="gated_merge",
    )(h, ya, yd, yx, w_gate, w_gate, w_gate, b_gate, b_gate, b_gate, w_a, w_d, w_x)


def _out_proj_kernel(z_ref, w_ref, x_ref, gpost_ref, gpre_ref, x1_ref, h2_ref):
    for r0 in range(0, z_ref.shape[0], EPILOGUE_ROWS):
        rows = slice(r0, r0 + EPILOGUE_ROWS)
        y = _dot(z_ref[rows, :], w_ref[...])
        x1 = x_ref[rows, :] + _rmsnorm_rows(y, gpost_ref[...])
        x1_ref[rows, :] = x1
        h2_ref[rows, :] = _rmsnorm_rows(x1, gpre_ref[...]).astype(h2_ref.dtype)


def out_proj_residual(z, w_o, x, g_post, g_next_pre):
    m, d = x.shape
    tm = _tile(m, 512)
    row = lambda: pl.BlockSpec((tm, d), lambda i: (i, 0))
    vec = lambda: pl.BlockSpec((1, d), lambda i: (0, 0))
    return pl.pallas_call(
        _out_proj_kernel,
        grid=(m // tm,),
        in_specs=[row(), pl.BlockSpec((d, d), lambda i: (0, 0)), row(), vec(), vec()],
        out_specs=[row(), row()],
        out_shape=[jax.ShapeDtypeStruct((m, d), F32), jax.ShapeDtypeStruct((m, d), F32)],
        compiler_params=_params(1),
        name="out_proj_residual",
    )(z, w_o, x, g_post.reshape(1, d), g_next_pre.reshape(1, d))


def _ffn_up_kernel(h_ref, wg_ref, wv_ref, cwg_ref, cwv_ref, o_ref, carry_g, carry_v, *, tiles_per_seq):
    i = pl.program_id(0)
    j = pl.program_id(1)
    start = i % tiles_per_seq == 0
    prev_g = _load_carry(carry_g, j, start)
    prev_v = _load_carry(carry_v, j, start)
    for r0 in range(0, h_ref.shape[0], FFN_UP_ROWS):
        h = h_ref[r0:r0 + FFN_UP_ROWS, :]
        zg = _dot(h, wg_ref[...])
        zv = _dot(h, wv_ref[...])
        gate = _causal_conv3(zg, prev_g, cwg_ref[...])
        val = _causal_conv3(zv, prev_v, cwv_ref[...])
        o_ref[r0:r0 + FFN_UP_ROWS, :] = (gate * jax.nn.sigmoid(gate) * val).astype(o_ref.dtype)
        prev_g = zg[FFN_UP_ROWS - SUBLANES:, :]
        prev_v = zv[FFN_UP_ROWS - SUBLANES:, :]
    carry_g[j] = prev_g
    carry_v[j] = prev_v


def ffn_up(h, w_up, conv_w, seq):
    m, d = h.shape
    d_ff = w_up.shape[1] // 2
    tm = _tile(seq, 2048)
    tn = 512
    nj = d_ff // tn
    kern = functools.partial(_ffn_up_kernel, tiles_per_seq=seq // tm)
    return pl.pallas_call(
        kern,
        grid=(m // tm, nj),
        in_specs=[pl.BlockSpec((tm, d), lambda i, j: (i, 0)),
                  pl.BlockSpec((d, tn), lambda i, j: (0, j)),
                  pl.BlockSpec((d, tn), lambda i, j: (0, nj + j)),
                  pl.BlockSpec((CONV_K, tn), lambda i, j: (0, j)),
                  pl.BlockSpec((CONV_K, tn), lambda i, j: (0, nj + j))],
        out_specs=pl.BlockSpec((tm, tn), lambda i, j: (i, j)),
        out_shape=jax.ShapeDtypeStruct((m, d_ff), BF16),
        scratch_shapes=[pltpu.VMEM((nj, SUBLANES, tn), F32), pltpu.VMEM((nj, SUBLANES, tn), F32)],
        compiler_params=_params(2),
        name="ffn_up",
    )(h, w_up, w_up, conv_w, conv_w)


def _ffn_down_kernel(a_ref, w_ref, x_ref, g_ref, o_ref):
    for r0 in range(0, a_ref.shape[0], EPILOGUE_ROWS):
        rows = slice(r0, r0 + EPILOGUE_ROWS)
        y = _dot(a_ref[rows, :], w_ref[...])
        o_ref[rows, :] = x_ref[rows, :] + _rmsnorm_rows(y, g_ref[...])


def ffn_down(a, w_down, x, g_post):
    m, d = x.shape
    d_ff = a.shape[1]
    tm = _tile(m, 512)
    row = lambda: pl.BlockSpec((tm, d), lambda i: (i, 0))
    return pl.pallas_call(
        _ffn_down_kernel,
        grid=(m // tm,),
        in_specs=[pl.BlockSpec((tm, d_ff), lambda i: (i, 0)),
                  pl.BlockSpec((d_ff, d), lambda i: (0, 0)),
                  row(), pl.BlockSpec((1, d), lambda i: (0, 0))],
        out_specs=row(),
        out_shape=jax.ShapeDtypeStruct((m, d), F32),
        compiler_params=_params(1),
        name="ffn_down",
    )(a, w_down, x, g_post.reshape(1, d))


def kernel(x, mem, g_mix_pre, w_in, w_gate, b_gate, conv_w, w_conv_out, lambda_q1, lambda_k1, lambda_q2, lambda_k2, g_diff_sub, w_diff_out, g_mem, w_mem_kv, w_cross_out, w_o, g_mix_post, g_ffn_pre, w_up, ffn_conv_w, w_down, g_ffn_post):
    bsz, seq, d = x.shape
    mem_len = mem.shape[1]
    depth = w_in.shape[0]
    m = bsz * seq
    q_lo = N_BRANCH * CONV_WIDTH
    qkvx_width = 2 * DIFF_QK_WIDTH + DIFF_WIDTH + CROSS_WIDTH
    qkvx_scale = jnp.where(jnp.arange(qkvx_width) < DIFF_QK_WIDTH,
                           LOG2E * DIFF_QK_DIM ** -0.5, 1.0).astype(F32)
    kv_scale = jnp.ones((2 * CROSS_WIDTH,), F32)

    xf = x.reshape(m, d)
    for l in range(depth):
        lambda_init = 0.8 - 0.6 * math.exp(-0.3 * l)
        w_in_b = w_in[l].astype(BF16)
        ya, h = conv_mixer(xf, g_mix_pre[l], w_in_b, conv_w[l], seq)
        qkvx = project(h, w_in_b, qkvx_scale, q_lo // 1024).reshape(bsz, seq, qkvx_width)

        lam_rows = jnp.stack([lambda_q1[l], lambda_k1[l], lambda_q2[l], lambda_k2[l]])
        yd = diff_attention(qkvx, lam_rows, g_diff_sub[l], bsz, seq, lambda_init)

        mem_n = rmsnorm_bf16(mem.reshape(bsz * mem_len, d), g_mem[l])
        kv = project(mem_n, w_mem_kv[l].astype(BF16), kv_scale, 0, tm_want=bsz * mem_len)
        yx = cross_attention(qkvx, kv.reshape(bsz, mem_len, 2 * CROSS_WIDTH), bsz, seq)

        merged = gated_merge(h, ya, yd.reshape(m, DIFF_WIDTH), yx.reshape(m, CROSS_WIDTH),
                             w_gate[l].astype(BF16), b_gate[l].reshape(1, -1),
                             w_conv_out[l].astype(BF16), w_diff_out[l].astype(BF16),
                             w_cross_out[l].astype(BF16))
        xf, h2 = out_proj_residual(merged, w_o[l].astype(BF16), xf, g_mix_post[l], g_ffn_pre[l])

        act = ffn_up(h2, w_up[l].astype(BF16), ffn_conv_w[l], seq)
        xf = ffn_down(act, w_down[l].astype(BF16), xf, g_ffn_post[l])
    return xf.reshape(bsz, seq, d)
```

```python
import functools
import math

import jax
import jax.numpy as jnp
from jax import lax
from jax.experimental import pallas as pl
from jax.experimental.pallas import tpu as pltpu

EPS = 1e-6
CHUNK = 64
CONV_WIDTH = 1024
CONV_K = 3
DIFF_HEADS = 8
DIFF_QK_DIM = 64
DIFF_V_DIM = 2 * DIFF_QK_DIM
DIFF_QK_WIDTH = DIFF_HEADS * 2 * DIFF_QK_DIM
DIFF_WIDTH = DIFF_HEADS * DIFF_V_DIM
CROSS_HEADS = 4
CROSS_HEAD_DIM = 256
CROSS_WIDTH = CROSS_HEADS * CROSS_HEAD_DIM
N_BRANCH = 3
LOG2E = math.log2(math.e)
MAX_EXCESS = 30.0
GROUP = 32

SUBLANES = 8
BF16_ROWS = 16
FFN_UP_ROWS = 128
EPILOGUE_ROWS = 256
VMEM_LIMIT = 56 * 1024 * 1024

F32 = jnp.float32
BF16 = jnp.bfloat16


def _params(n_axes):
    return pltpu.CompilerParams(
        dimension_semantics=("arbitrary",) * n_axes, vmem_limit_bytes=VMEM_LIMIT)


def _dot(a, b):
    return lax.dot_general(a, b, (((1,), (0,)), ((), ())), preferred_element_type=F32)


def _dot_nt(a, b):
    return lax.dot_general(a, b, (((1,), (1,)), ((), ())), preferred_element_type=F32)


def _tile(n, want):
    t = min(n, want)
    assert n % t == 0, (n, want)
    return t


def _rmsnorm_rows(x, g):
    ms = jnp.mean(x * x, axis=-1, keepdims=True)
    return x * lax.rsqrt(ms + EPS) * g


def _rmsnorm_kernel(x_ref, g_ref, o_ref):
    o_ref[...] = _rmsnorm_rows(x_ref[...], g_ref[...]).astype(o_ref.dtype)


def rmsnorm_bf16(x, g):
    m, d = x.shape
    tm = _tile(m, 512)
    return pl.pallas_call(
        _rmsnorm_kernel,
        grid=(m // tm,),
        in_specs=[pl.BlockSpec((tm, d), lambda i: (i, 0)),
                  pl.BlockSpec((1, d), lambda i: (0, 0))],
        out_specs=pl.BlockSpec((tm, d), lambda i: (i, 0)),
        out_shape=jax.ShapeDtypeStruct((m, d), BF16),
        compiler_params=_params(1),
        name="rmsnorm",
    )(x, g.reshape(1, d))


def _shift_rows(u, prev, s):
    rolled = pltpu.roll(u, s, axis=0)
    rprev = pltpu.roll(prev, s, axis=0)
    rows = lax.broadcasted_iota(jnp.int32, prev.shape, 0)
    first = jnp.where(rows < s, rprev, rolled[0:SUBLANES])
    return jnp.concatenate([first, rolled[SUBLANES:]], axis=0)


def _causal_conv3(u, prev, w):
    return (w[2:3] * u + w[1:2] * _shift_rows(u, prev, 1) + w[0:1] * _shift_rows(u, prev, 2))


def _load_carry(carry_ref, j, at_seq_start):
    @pl.when(at_seq_start)
    def _():
        carry_ref[j] = jnp.zeros(carry_ref.shape[1:], carry_ref.dtype)
    return carry_ref[j]


def _conv_mixer_kernel(x_ref, g_ref, wb_ref, wc_ref, wv_ref, cw_ref, o_ref, h_ref, carry_ref, *, tiles_per_seq):
    i = pl.program_id(0)
    j = pl.program_id(1)

    @pl.when(j == 0)
    def _():
        h_ref[...] = _rmsnorm_rows(x_ref[...], g_ref[...]).astype(h_ref.dtype)

    h = h_ref[...]
    u = _dot(h, wc_ref[...]) * _dot(h, wv_ref[...])
    prev = _load_carry(carry_ref, j, i % tiles_per_seq == 0)
    conv = _causal_conv3(u, prev, cw_ref[...])
    carry_ref[j] = u[u.shape[0] - SUBLANES:, :]
    o_ref[...] = (_dot(h, wb_ref[...]) * conv).astype(o_ref.dtype)


def conv_mixer(x, g_pre, w_in, conv_w, seq):
    m, d = x.shape
    tm = _tile(seq, 1024)
    tn = 512
    nj = CONV_WIDTH // tn
    kern = functools.partial(_conv_mixer_kernel, tiles_per_seq=seq // tm)
    return pl.pallas_call(
        kern,
        grid=(m // tm, nj),
        in_specs=[pl.BlockSpec((tm, d), lambda i, j: (i, 0)),
                  pl.BlockSpec((1, d), lambda i, j: (0, 0)),
                  pl.BlockSpec((d, tn), lambda i, j: (0, j)),
                  pl.BlockSpec((d, tn), lambda i, j: (0, nj + j)),
                  pl.BlockSpec((d, tn), lambda i, j: (0, 2 * nj + j)),
                  pl.BlockSpec((CONV_K, tn), lambda i, j: (0, j))],
        out_specs=[pl.BlockSpec((tm, tn), lambda i, j: (i, j)),
                   pl.BlockSpec((tm, d), lambda i, j: (i, 0))],
        out_shape=[jax.ShapeDtypeStruct((m, CONV_WIDTH), BF16), jax.ShapeDtypeStruct((m, d), BF16)],
        scratch_shapes=[pltpu.VMEM((nj, SUBLANES, tn), F32)],
        compiler_params=_params(2),
        name="conv_mixer",
    )(x, g_pre.reshape(1, d), w_in, w_in, w_in, conv_w)


def _proj_kernel(h_ref, w_ref, scale_ref, o_ref):
    o_ref[...] = (_dot(h_ref[...], w_ref[...]) * scale_ref[...]).astype(o_ref.dtype)


def project(h, w, col_scale, col_block0, tm_want=2048, tn=1024):
    m, d = h.shape
    n_out = col_scale.shape[0]
    tm = _tile(m, tm_want)
    return pl.pallas_call(
        _proj_kernel,
        grid=(m // tm, n_out // tn),
        in_specs=[pl.BlockSpec((tm, d), lambda i, j: (i, 0)),
                  pl.BlockSpec((d, tn), lambda i, j: (0, col_block0 + j)),
                  pl.BlockSpec((1, tn), lambda i, j: (0, j))],
        out_specs=pl.BlockSpec((tm, tn), lambda i, j: (i, j)),
        out_shape=jax.ShapeDtypeStruct((m, n_out), BF16),
        compiler_params=_params(2),
        name="project",
    )(h, w, col_scale.reshape(1, n_out))


def _col_max(s):
    r = s.shape[0]
    while r > SUBLANES:
        r //= 2
        s = jnp.maximum(s[:r], s[r:])
    return jnp.max(s, axis=0, keepdims=True)


def _diff_attn_kernel(lam_ref, q_ref, k_ref, v_ref, gsub_ref, o_ref,
                      vt_ref, s_ref, p_ref, alpha_ref, m_ref, excess_ref, acc_ref, *, tq, tk, lambda_init):
    qi = pl.program_id(2)
    seq = k_ref.shape[1]
    dv = DIFF_V_DIM
    assert tq == 2 * tk

    @pl.when(qi == 0)
    def _():
        vt_ref[dv:, :] = jnp.ones((BF16_ROWS, seq), BF16)

        def body(c, carry):
            r = pl.multiple_of(c * tq, tq)
            vt_ref[0:dv, pl.ds(r, tq)] = v_ref[0, pl.ds(r, tq), :].astype(F32).T.astype(BF16)
            return carry
        lax.fori_loop(0, seq // tq, body, 0)

    q = q_ref[0].astype(F32)
    lane = lax.broadcasted_iota(jnp.int32, q.shape, 1)
    qz = (jnp.where(lane < DIFF_QK_DIM, q, 0.0).astype(BF16),
          jnp.where(lane >= DIFF_QK_DIM, q, 0.0).astype(BF16))

    def k_tile(t):
        return k_ref[0, pl.ds(pl.multiple_of(t * tk, tk), tk), :]

    def vt_tile(t):
        return vt_ref[:, pl.ds(pl.multiple_of(t * tk, tk), tk)]

    key_chunk = lax.broadcasted_iota(jnp.int32, (tk, tq), 0) // CHUNK
    qry_chunk = lax.broadcasted_iota(jnp.int32, (tk, tq), 1) // CHUNK
    diag_masks = (key_chunk <= qry_chunk, key_chunk + tk // CHUNK <= qry_chunk)

    def exact_tile(t):
        k_t = k_tile(t)
        vt_t = vt_tile(t)
        for c in range(2):
            s = _dot_nt(k_t, qz[c])
            m_old = m_ref[c]
            m_new = jnp.maximum(m_old, _col_max(s))
            p = jnp.exp2(s - m_new).astype(BF16)
            acc_ref[c] = jnp.exp2(m_old - m_new) * acc_ref[c] + _dot(vt_t, p)
            m_ref[c] = m_new

    def exact_diagonal():
        k_d = [k_tile(2 * qi + d) for d in range(2)]
        vt_d = [vt_tile(2 * qi + d) for d in range(2)]
        for c in range(2):
            s = [jnp.where(diag_masks[d], _dot_nt(k_d[d], qz[c]), -jnp.inf) for d in range(2)]
            m = jnp.maximum(_col_max(s[0]), _col_max(s[1]))
            acc_ref[c] = (_dot(vt_d[0], jnp.exp2(s[0] - m).astype(BF16))
                          + _dot(vt_d[1], jnp.exp2(s[1] - m).astype(BF16)))
            m_ref[c] = m

    def fast_group(t0, n, masks=None):
        assert 2 <= n <= GROUP
        slots = range(GROUP - n, GROUP)
        m_frozen = [m_ref[c] for c in range(2)]
        tile_max = [None, None]
        if masks is None:
            k_next = k_tile(t0 + n)
            s_next = [_dot_nt(k_next, qz[c]) for c in range(2)]
        for c in range(2):
            s = s_ref[c]
            if masks is not None:
                s = jnp.where(masks[0], s, -jnp.inf)
            tile_max[c] = _col_max(s)
            p_ref[slots[0], c] = jnp.exp2(s - m_frozen[c]).astype(BF16)
            if masks is None:
                s_ref[c] = s_next[c]
        vt_pending = vt_tile(jnp.maximum(t0 - 1, 0))
        acc = [alpha_ref[c] * (acc_ref[c] + _dot(vt_pending, p_ref[GROUP - 1, c])) for c in range(2)]
        for i in range(1, n):
            k_t = k_tile(t0 + i)
            for c in range(2):
                s = _dot_nt(k_t, qz[c])
                if masks is not None:
                    s = jnp.where(masks[i], s, -jnp.inf)
                tile_max[c] = jnp.maximum(tile_max[c], _col_max(s))
                p_ref[slots[i], c] = jnp.exp2(s - m_frozen[c]).astype(BF16)
            vt_t = vt_tile(t0 + i - 1)
            for c in range(2):
                acc[c] = acc[c] + _dot(vt_t, p_ref[slots[i - 1], c])
        if masks is not None:
            vt_t = vt_tile(t0 + n - 1)
            for c in range(2):
                acc[c] = acc[c] + _dot(vt_t, p_ref[slots[n - 1], c])
        for c in range(2):
            acc_ref[c] = acc[c]
            m_new = jnp.maximum(m_frozen[c], tile_max[c])
            alpha_ref[c] = jnp.exp2(m_frozen[c] - m_new)
            excess_ref[c] = jnp.maximum(excess_ref[c], tile_max[c] - m_frozen[c])
            m_ref[c] = m_new

    for c in range(2):
        s0 = _dot_nt(k_tile(0), qz[c])
        s_ref[c] = s0
        m_ref[c] = _col_max(s0[0:CHUNK])
    acc_ref[...] = jnp.zeros(acc_ref.shape, F32)
    p_ref[GROUP - 1] = jnp.zeros(p_ref.shape[1:], BF16)
    alpha_ref[...] = jnp.ones(alpha_ref.shape, F32)
    excess_ref[...] = jnp.zeros(excess_ref.shape, F32)
    n_tiles = 2 * qi
    rem = n_tiles % GROUP
    size = 2
    while size < GROUP:
        @pl.when(rem & size > 0)
        def _(size=size):
            fast_group(rem & (size - 1), size)
        size *= 2

    def group(j, carry):
        fast_group(rem + GROUP * j, GROUP)
        return carry
    lax.fori_loop(0, n_tiles // GROUP, group, 0)
    fast_group(n_tiles, 2, diag_masks)

    worst = jnp.max(jnp.maximum(excess_ref[0], excess_ref[1]), axis=-1, keepdims=True)

    @pl.when(worst[0, 0] > MAX_EXCESS)
    def _():
        exact_diagonal()

        def body(t, carry):
            exact_tile(t)
            return carry
        lax.fori_loop(0, 2 * qi, body, 0)

    lv = lam_ref[...]
    lam = (jnp.exp(jnp.sum(lv[0:1] * lv[1:2], axis=-1, keepdims=True))
           - jnp.exp(jnp.sum(lv[2:3] * lv[3:4], axis=-1, keepdims=True)) + lambda_init)
    a1 = acc_ref[0]
    a2 = acc_ref[1]
    o = a1[0:dv] / a1[dv:dv + 1] - lam * (a2[0:dv] / a2[dv:dv + 1])
    ms = jnp.mean(o * o, axis=0, keepdims=True)
    y = o * lax.rsqrt(ms + EPS) * gsub_ref[...]
    y = y * (1.0 - lambda_init)
    o_ref[0] = y.T.astype(o_ref.dtype)


def diff_attention(qkvx, lam_rows, g_sub, bsz, seq, lambda_init):
    tq = _tile(seq, 512)
    tk = tq // 2
    nq = seq // tq
    dv = DIFF_V_DIM
    kern = functools.partial(_diff_attn_kernel, tq=tq, tk=tk, lambda_init=lambda_init)
    return pl.pallas_call(
        kern,
        grid=(bsz, DIFF_HEADS, nq),
        in_specs=[pl.BlockSpec((4, DIFF_QK_DIM), lambda b, h, i: (0, 0)),
                  pl.BlockSpec((1, tq, dv), lambda b, h, i: (b, i, h)),
                  pl.BlockSpec((1, seq, dv), lambda b, h, i: (b, 0, DIFF_HEADS + h)),
                  pl.BlockSpec((1, seq, dv), lambda b, h, i: (b, 0, 2 * DIFF_HEADS + h)),
                  pl.BlockSpec((dv, 1), lambda b, h, i: (0, 0))],
        out_specs=pl.BlockSpec((1, tq, dv), lambda b, h, i: (b, i, h)),
        out_shape=jax.ShapeDtypeStruct((bsz, seq, DIFF_WIDTH), BF16),
        scratch_shapes=[pltpu.VMEM((dv + BF16_ROWS, seq), BF16),
                        pltpu.VMEM((2, tk, tq), F32),
                        pltpu.VMEM((GROUP, 2, tk, tq), BF16),
                        pltpu.VMEM((2, 1, tq), F32),
                        pltpu.VMEM((2, 1, tq), F32),
                        pltpu.VMEM((2, 1, tq), F32),
                        pltpu.VMEM((2, dv + BF16_ROWS, tq), F32)],
        compiler_params=_params(3),
        name="diff_attention",
    )(lam_rows, qkvx, qkvx, qkvx, g_sub.reshape(dv, 1))


def _cross_attn_kernel(xq_ref, kv_ref, o_ref):
    hd = CROSS_HEAD_DIM
    for h in range(CROSS_HEADS):
        q = xq_ref[0, :, h * hd:(h + 1) * hd]
        k = kv_ref[0, :, h * hd:(h + 1) * hd]
        v = kv_ref[0, :, CROSS_WIDTH + h * hd:CROSS_WIDTH + (h + 1) * hd]
        s = _dot_nt(q, k) * (hd ** -0.5)
        p = jnp.exp(s - jnp.max(s, axis=-1, keepdims=True))
        l = jnp.sum(p, axis=-1, keepdims=True)
        o_ref[0, :, h * hd:(h + 1) * hd] = (_dot(p.astype(BF16), v) / l).astype(o_ref.dtype)


def cross_attention(qkvx, kv, bsz, seq):
    tm = _tile(seq, 512)
    mem_len = kv.shape[1]
    return pl.pallas_call(
        _cross_attn_kernel,
        grid=(bsz, seq // tm),
        in_specs=[pl.BlockSpec((1, tm, CROSS_WIDTH), lambda b, i: (b, i, 3)),
                  pl.BlockSpec((1, mem_len, 2 * CROSS_WIDTH), lambda b, i: (b, 0, 0))],
        out_specs=pl.BlockSpec((1, tm, CROSS_WIDTH), lambda b, i: (b, i, 0)),
        out_shape=jax.ShapeDtypeStruct((bsz, seq, CROSS_WIDTH), BF16),
        compiler_params=_params(2),
        name="cross_attention",
    )(qkvx, kv)


def _merge_kernel(h_ref, ya_ref, yd_ref, yx_ref, wga_ref, wgb_ref, wgc_ref,
                  ba_ref, bb_ref, bc_ref, wa_ref, wd_ref, wx_ref, o_ref):
    for r0 in range(0, h_ref.shape[0], EPILOGUE_ROWS):
        rows = slice(r0, r0 + EPILOGUE_ROWS)
        h = h_ref[rows, :]
        merged = jax.nn.sigmoid(_dot(h, wga_ref[...]) + ba_ref[...]) * _dot(ya_ref[rows, :], wa_ref[...])
        merged += jax.nn.sigmoid(_dot(h, wgb_ref[...]) + bb_ref[...]) * _dot(yd_ref[rows, :], wd_ref[...])
        merged += jax.nn.sigmoid(_dot(h, wgc_ref[...]) + bc_ref[...]) * _dot(yx_ref[rows, :], wx_ref[...])
        o_ref[rows, :] = merged.astype(o_ref.dtype)


def gated_merge(h, ya, yd, yx, w_gate, b_gate, w_a, w_d, w_x):
    m, d = h.shape
    tm = _tile(m, 1024)
    tn = 512
    nj = d // tn
    act = lambda width: pl.BlockSpec((tm, width), lambda i, j: (i, 0))
    gate_w = lambda br: pl.BlockSpec((d, tn), lambda i, j: (0, br * nj + j))
    gate_b = lambda br: pl.BlockSpec((1, tn), lambda i, j: (0, br * nj + j))
    out_w = lambda width: pl.BlockSpec((width, tn), lambda i, j: (0, j))
    return pl.pallas_call(
        _merge_kernel,
        grid=(m // tm, nj),
        in_specs=[act(d), act(CONV_WIDTH), act(DIFF_WIDTH), act(CROSS_WIDTH),
                  gate_w(0), gate_w(1), gate_w(2), gate_b(0), gate_b(1), gate_b(2),
                  out_w(CONV_WIDTH), out_w(DIFF_WIDTH), out_w(CROSS_WIDTH)],
        out_specs=pl.BlockSpec((tm, tn), lambda i, j: (i, j)),
        out_shape=jax.ShapeDtypeStruct((m, d), BF16),
        compiler_params=_params(2),
        name="gated_merge",
    )(h, ya, yd, yx, w_gate, w_gate, w_gate, b_gate, b_gate, b_gate, w_a, w_d, w_x)


def _out_proj_kernel(z_ref, w_ref, x_ref, gpost_ref, gpre_ref, x1_ref, h2_ref):
    for r0 in range(0, z_ref.shape[0], EPILOGUE_ROWS):
        rows = slice(r0, r0 + EPILOGUE_ROWS)
        y = _dot(z_ref[rows, :], w_ref[...])
        x1 = x_ref[rows, :] + _rmsnorm_rows(y, gpost_ref[...])
        x1_ref[rows, :] = x1
        h2_ref[rows, :] = _rmsnorm_rows(x1, gpre_ref[...]).astype(h2_ref.dtype)


def out_proj_residual(z, w_o, x, g_post, g_next_pre):
    m, d = x.shape
    tm = _tile(m, 512)
    row = lambda: pl.BlockSpec((tm, d), lambda i: (i, 0))
    vec = lambda: pl.BlockSpec((1, d), lambda i: (0, 0))
    return pl.pallas_call(
        _out_proj_kernel,
        grid=(m // tm,),
        in_specs=[row(), pl.BlockSpec((d, d), lambda i: (0, 0)), row(), vec(), vec()],
        out_specs=[row(), row()],
        out_shape=[jax.ShapeDtypeStruct((m, d), F32), jax.ShapeDtypeStruct((m, d), F32)],
        compiler_params=_params(1),
        name="out_proj_residual",
    )(z, w_o, x, g_post.reshape(1, d), g_next_pre.reshape(1, d))


def _ffn_up_kernel(h_ref, wg_ref, wv_ref, cwg_ref, cwv_ref, o_ref, carry_g, carry_v, *, tiles_per_seq):
    i = pl.program_id(0)
    j = pl.program_id(1)
    start = i % tiles_per_seq == 0
    prev_g = _load_carry(carry_g, j, start)
    prev_v = _load_carry(carry_v, j, start)
    for r0 in range(0, h_ref.shape[0], FFN_UP_ROWS):
        h = h_ref[r0:r0 + FFN_UP_ROWS, :]
        zg = _dot(h, wg_ref[...])
        zv = _dot(h, wv_ref[...])
        gate = _causal_conv3(zg, prev_g, cwg_ref[...])
        val = _causal_conv3(zv, prev_v, cwv_ref[...])
        o_ref[r0:r0 + FFN_UP_ROWS, :] = (gate * jax.nn.sigmoid(gate) * val).astype(o_ref.dtype)
        prev_g = zg[FFN_UP_ROWS - SUBLANES:, :]
        prev_v = zv[FFN_UP_ROWS - SUBLANES:, :]
    carry_g[j] = prev_g
    carry_v[j] = prev_v


def ffn_up(h, w_up, conv_w, seq):
    m, d = h.shape
    d_ff = w_up.shape[1] // 2
    tm = _tile(seq, 2048)
    tn = 512
    nj = d_ff // tn
    kern = functools.partial(_ffn_up_kernel, tiles_per_seq=seq // tm)
    return pl.pallas_call(
        kern,
        grid=(m // tm, nj),
        in_specs=[pl.BlockSpec((tm, d), lambda i, j: (i, 0)),
                  pl.BlockSpec((d, tn), lambda i, j: (0, j)),
                  pl.BlockSpec((d, tn), lambda i, j: (0, nj + j)),
                  pl.BlockSpec((CONV_K, tn), lambda i, j: (0, j)),
                  pl.BlockSpec((CONV_K, tn), lambda i, j: (0, nj + j))],
        out_specs=pl.BlockSpec((tm, tn), lambda i, j: (i, j)),
        out_shape=jax.ShapeDtypeStruct((m, d_ff), BF16),
        scratch_shapes=[pltpu.VMEM((nj, SUBLANES, tn), F32), pltpu.VMEM((nj, SUBLANES, tn), F32)],
        compiler_params=_params(2),
        name="ffn_up",
    )(h, w_up, w_up, conv_w, conv_w)


def _ffn_down_kernel(a_ref, w_ref, x_ref, g_ref, o_ref):
    for r0 in range(0, a_ref.shape[0], EPILOGUE_ROWS):
        rows = slice(r0, r0 + EPILOGUE_ROWS)
        y = _dot(a_ref[rows, :], w_ref[...])
        o_ref[rows, :] = x_ref[rows, :] + _rmsnorm_rows(y, g_ref[...])


def ffn_down(a, w_down, x, g_post):
    m, d = x.shape
    d_ff = a.shape[1]
    tm = _tile(m, 512)
    row = lambda: pl.BlockSpec((tm, d), lambda i: (i, 0))
    return pl.pallas_call(
        _ffn_down_kernel,
        grid=(m // tm,),
        in_specs=[pl.BlockSpec((tm, d_ff), lambda i: (i, 0)),
                  pl.BlockSpec((d_ff, d), lambda i: (0, 0)),
                  row(), pl.BlockSpec((1, d), lambda i: (0, 0))],
        out_specs=row(),
        out_shape=jax.ShapeDtypeStruct((m, d), F32),
        compiler_params=_params(1),
        name="ffn_down",
    )(a, w_down, x, g_post.reshape(1, d))


def kernel(x, mem, g_mix_pre, w_in, w_gate, b_gate, conv_w, w_conv_out, lambda_q1, lambda_k1, lambda_q2, lambda_k2, g_diff_sub, w_diff_out, g_mem, w_mem_kv, w_cross_out, w_o, g_mix_post, g_ffn_pre, w_up, ffn_conv_w, w_down, g_ffn_post):
    bsz, seq, d = x.shape
    mem_len = mem.shape[1]
    depth = w_in.shape[0]
    m = bsz * seq
    q_lo = N_BRANCH * CONV_WIDTH
    qkvx_width = 2 * DIFF_QK_WIDTH + DIFF_WIDTH + CROSS_WIDTH
    qkvx_scale = jnp.where(jnp.arange(qkvx_width) < DIFF_QK_WIDTH,
                           LOG2E * DIFF_QK_DIM ** -0.5, 1.0).astype(F32)
    kv_scale = jnp.ones((2 * CROSS_WIDTH,), F32)

    xf = x.reshape(m, d)
    for l in range(depth):
        lambda_init = 0.8 - 0.6 * math.exp(-0.3 * l)
        w_in_b = w_in[l].astype(BF16)
        ya, h = conv_mixer(xf, g_mix_pre[l], w_in_b, conv_w[l], seq)
        qkvx = project(h, w_in_b, qkvx_scale, q_lo // 1024).reshape(bsz, seq, qkvx_width)

        lam_rows = jnp.stack([lambda_q1[l], lambda_k1[l], lambda_q2[l], lambda_k2[l]])
        yd = diff_attention(qkvx, lam_rows, g_diff_sub[l], bsz, seq, lambda_init)

        mem_n = rmsnorm_bf16(mem.reshape(bsz * mem_len, d), g_mem[l])
        kv = project(mem_n, w_mem_kv[l].astype(BF16), kv_scale, 0, tm_want=bsz * mem_len)
        yx = cross_attention(qkvx, kv.reshape(bsz, mem_len, 2 * CROSS_WIDTH), bsz, seq)

        merged = gated_merge(h, ya, yd.reshape(m, DIFF_WIDTH), yx.reshape(m, CROSS_WIDTH),
                             w_gate[l].astype(BF16), b_gate[l].reshape(1, -1),
                             w_conv_out[l].astype(BF16), w_diff_out[l].astype(BF16),
                             w_cross_out[l].astype(BF16))
        xf, h2 = out_proj_residual(merged, w_o[l].astype(BF16), xf, g_mix_post[l], g_ffn_pre[l])

        act = ffn_up(h2, w_up[l].astype(BF16), ffn_conv_w[l], seq)
        xf = ffn_down(act, w_down[l].astype(BF16), xf, g_ffn_post[l])
    return xf.reshape(bsz, seq, d)
```
